```python
import jax
import jax.numpy as jnp
from jax import lax
import numpy as np

D_MODEL = 1024
BATCH = 8
SEQ = 4096
DEPTH = 2

HEAD_DIM = 64
ROT_DIM = HEAD_DIM // 4
ROPE_THETA = 500000.0
BLOCK_Q = 128
NORM_EPS = 1e-6
D_FF = 2816

DIL_PAIRS = ((128, 1), (512, 4), (2048, 16))
A_GROUPS = len(DIL_PAIRS)
A_SLOTS = 6

B_HEADS = 8
B_KV = 2
B_REP = B_HEADS // B_KV
CMP_BLOCK = 32
CMP_STRIDE = 16
CMP_HIDDEN = 2 * HEAD_DIM
SEL_BLOCK = 64
N_SELECT = 16
WINDOW = 512
FORCE_SCORE = 1e4

C_HEADS = 6

A_WIDTH = A_SLOTS * HEAD_DIM
B_WIDTH = B_HEADS * HEAD_DIM
C_WIDTH = C_HEADS * HEAD_DIM
MIX_WIDTH = A_WIDTH + B_WIDTH + C_WIDTH
A_IN = 3 * A_GROUPS * A_WIDTH
B_Q_IN = B_WIDTH
B_KV_IN = 3 * 2 * B_KV * HEAD_DIM
B_GATE_IN = 3 * B_HEADS
B_IN = B_Q_IN + B_KV_IN + B_GATE_IN
C_IN = 3 * C_WIDTH
IN_WIDTH = A_IN + B_IN + C_IN

kernel_name = 'hybrid_dilated_nsa_stickbreak_macaron'


def rms_norm(x, g):
    xf = x.astype(jnp.float32)
    y = xf * lax.rsqrt(jnp.mean(xf * xf, axis=-1, keepdims=True) + NORM_EPS)
    return (y * g.astype(jnp.float32)).astype(x.dtype)


def swiglu(x, w1, w3, w2):
    return (jax.nn.silu(x @ w1) * (x @ w3)) @ w2


def rope_tables(positions):
    inv = ROPE_THETA ** (-jnp.arange(0, ROT_DIM, 2, dtype=jnp.float32) / ROT_DIM)
    ang = positions.astype(jnp.float32)[..., None] * inv
    return jnp.cos(ang), jnp.sin(ang)


def apply_rope(x, cos, sin):
    half = ROT_DIM // 2
    shp = cos.shape[:2] + (1,) * (x.ndim - 3) + (half,)
    c, s = cos.reshape(shp), sin.reshape(shp)
    xr = x[..., :ROT_DIM].astype(jnp.float32)
    x1, x2 = xr[..., :half], xr[..., half:]
    rot = jnp.concatenate([x1 * c - x2 * s, x2 * c + x1 * s], axis=-1).astype(x.dtype)
    return jnp.concatenate([rot, x[..., ROT_DIM:]], axis=-1)


def masked_softmax(s, mask):
    s = jnp.where(mask, s, -jnp.inf)
    m = jnp.max(s, axis=-1, keepdims=True)
    m = jnp.where(jnp.isfinite(m), m, 0.0)
    p = jnp.exp(s - m)
    l = jnp.sum(p, axis=-1, keepdims=True)
    l_safe = jnp.where(l > 0, l, 1.0)
    return p / l_safe, (m + jnp.log(l_safe))[..., 0]


def band_blocks(x, blk, n_prev):
    n, L = x.shape[:2]
    nb = L // blk
    xb = x.reshape((n, nb, blk) + x.shape[2:])
    xp = jnp.pad(xb, [(0, 0), (n_prev, 0)] + [(0, 0)] * (xb.ndim - 2))
    return jnp.concatenate([xp[:, i:i + nb] for i in range(n_prev + 1)], axis=2)


def banded_attention(q, k, v, blk, n_prev, max_dist):
    n, L, G, R, dh = q.shape
    nb = L // blk
    width = (n_prev + 1) * blk
    qb = q.reshape(n, nb, blk, G, R, dh)
    kb = band_blocks(k, blk, n_prev)
    vb = band_blocks(v, blk, n_prev)
    s = jnp.einsum('nbqgrd,nbkgd->nbgrqk', qb, kb, preferred_element_type=jnp.float32) * (dh ** -0.5)
    qi = jnp.arange(blk)[:, None] + n_prev * blk
    ki = jnp.arange(width)[None, :]
    dist = qi - ki
    key_pos = jnp.arange(nb)[:, None, None] * blk + ki[None] - n_prev * blk
    mask = (dist >= 0) & (dist <= max_dist) & (key_pos >= 0)
    p, lse = masked_softmax(s, mask[:, None, None])
    o = jnp.einsum('nbgrqk,nbkgd->nbqgrd', p, vb.astype(jnp.float32)).astype(q.dtype)
    return o.reshape(n, L, G, R, dh), lse.transpose(0, 1, 4, 2, 3).reshape(n, L, G, R)


def dilated_attention(q, k, v):
    B, S = q.shape[:2]
    outs, lses = [], []
    for g, (window, dil) in enumerate(DIL_PAIRS):
        steps = window // dil
        n_prev = -(-steps // BLOCK_Q)
        unit = dil * BLOCK_Q
        Sp = -(-S // unit) * unit
        M = Sp // dil

        def to_sub(t):
            t = jnp.pad(t, ((0, 0), (0, Sp - S), (0, 0), (0, 0)))
            t = t.reshape(B, M, dil, A_SLOTS, HEAD_DIM).transpose(0, 2, 1, 3, 4)
            return t.reshape(B * dil, M, A_SLOTS, HEAD_DIM)

        qs, ks, vs = to_sub(q[:, :, g]), to_sub(k[:, :, g]), to_sub(v[:, :, g])
        o, lse = banded_attention(qs[:, :, :, None], ks, vs, BLOCK_Q, n_prev, steps)
        o = o[:, :, :, 0].reshape(B, dil, M, A_SLOTS, HEAD_DIM).transpose(0, 2, 1, 3, 4)
        outs.append(o.reshape(B, Sp, A_SLOTS, HEAD_DIM)[:, :S])
        lse = lse[..., 0].reshape(B, dil, M, A_SLOTS).transpose(0, 2, 1, 3)
        lses.append(lse.reshape(B, Sp, A_SLOTS)[:, :S])
    alpha = jax.nn.softmax(jnp.stack(lses, axis=0), axis=0)
    o = jnp.sum(alpha[..., None] * jnp.stack(outs, axis=0).astype(jnp.float32), axis=0)
    return o.reshape(B, S, A_WIDTH).astype(q.dtype)


def nsa_compress(x, pe, w1, w2):
    B, S = x.shape[:2]
    n_chunk = S // CMP_STRIDE
    per = CMP_BLOCK // CMP_STRIDE
    n_cmp = n_chunk - per + 1
    c = x.reshape(B, n_chunk, CMP_STRIDE, B_KV, HEAD_DIM)
    blocks = jnp.concatenate([c[:, i:i + n_cmp] for i in range(per)], axis=2)
    blocks = blocks + pe[:, None, :]
    flat = blocks.transpose(0, 1, 3, 2, 4).reshape(B, n_cmp, B_KV, CMP_BLOCK * HEAD_DIM)
    return jax.nn.gelu(flat @ w1) @ w2


def nsa_attention(q, kv, gate_logits, cos, sin, pe_k, cw1_k, cw2_k, pe_v, cw1_v, cw2_v):
    B, S = q.shape[:2]
    scale = HEAD_DIM ** -0.5
    t = jnp.arange(S)
    per = CMP_BLOCK // CMP_STRIDE
    ratio = SEL_BLOCK // CMP_STRIDE

    kc = nsa_compress(kv[:, :, 0, 0], pe_k, cw1_k, cw2_k)
    vc = nsa_compress(kv[:, :, 0, 1], pe_v, cw1_v, cw2_v)
    n_cmp = kc.shape[1]
    qg = q.reshape(B, S, B_KV, B_REP, HEAD_DIM)
    s = jnp.einsum('bsgrd,bcgd->bgrsc', qg, kc, preferred_element_type=jnp.float32) * scale
    cmp_end = jnp.arange(n_cmp) * CMP_STRIDE + CMP_BLOCK - 1
    p_cmp, _ = masked_softmax(s, cmp_end[None, :] <= t[:, None])
    o_cmp = jnp.einsum('bgrsc,bcgd->bsgrd', p_cmp, vc.astype(jnp.float32))

    n_sel = S // SEL_BLOCK
    imp = jnp.sum(p_cmp, axis=2)
    imp = jnp.pad(imp, ((0, 0), (0, 0), (0, 0), (0, ratio * n_sel + ratio + per - n_cmp)))
    sel_score = imp[..., 0:ratio * n_sel:ratio] * 0.0
    for m in range(ratio):
        for n in range(per):
            sel_score = sel_score + imp[..., m + n:m + n + ratio * n_sel:ratio]
    j = jnp.arange(n_sel)[None, :]
    cur = (t // SEL_BLOCK)[:, None]
    valid = j * SEL_BLOCK <= t[:, None]
    forced = (j == 0) | (j == cur) | (j == cur - 1)
    sel_score = jnp.where(forced, FORCE_SCORE, jnp.where(valid, sel_score, -1.0))
    n_top = min(N_SELECT, n_sel)
    _, idx = lax.top_k(sel_score, n_top)

    qr = apply_rope(q, cos, sin).reshape(B, S, B_KV, B_REP, HEAD_DIM)
    k_slc = apply_rope(kv[:, :, 1, 0], cos, sin)

    def to_sel_blocks(x):
        x = x.reshape(B, n_sel, SEL_BLOCK, B_KV, HEAD_DIM).transpose(0, 3, 1, 2, 4)
        return x.reshape(B, B_KV, n_sel, SEL_BLOCK * HEAD_DIM)

    kb, vb = to_sel_blocks(k_slc), to_sel_blocks(kv[:, :, 1, 1])
    nq = S // BLOCK_Q
    q_steps = qr.reshape(B, nq, BLOCK_Q, B_KV, B_REP, HEAD_DIM).transpose(1, 0, 2, 3, 4, 5)
    idx_steps = idx.reshape(B, B_KV, nq, BLOCK_Q, n_top).transpose(2, 0, 1, 3, 4)
    t_steps = t.reshape(nq, BLOCK_Q)
    bi = jnp.arange(B)[:, None, None]
    gi = jnp.arange(B_KV)[None, :, None]

    def sel_step(args):
        qb, ib, tb = args
        flat = ib.reshape(B, B_KV, BLOCK_Q * n_top)
        kg = kb[bi, gi, flat].reshape(B, B_KV, BLOCK_Q, n_top * SEL_BLOCK, HEAD_DIM)
        vg = vb[bi, gi, flat].reshape(B, B_KV, BLOCK_Q, n_top * SEL_BLOCK, HEAD_DIM)
        kpos = (ib[..., None] * SEL_BLOCK + jnp.arange(SEL_BLOCK)).reshape(B, B_KV, BLOCK_Q, n_top * SEL_BLOCK)
        sc = jnp.einsum('bqgrd,bgqkd->bgrqk', qb, kg, preferred_element_type=jnp.float32) * scale
        p, _ = masked_softmax(sc, (kpos <= tb[:, None])[:, :, None])
        return jnp.einsum('bgrqk,bgqkd->bqgrd', p, vg.astype(jnp.float32))

    o_sel = lax.map(sel_step, (q_steps, idx_steps, t_steps))
    o_sel = o_sel.transpose(1, 0, 2, 3, 4, 5).reshape(B, S, B_KV, B_REP, HEAD_DIM)

    k_win = apply_rope(kv[:, :, 2, 0], cos, sin)
    o_win, _ = banded_attention(qr, k_win, kv[:, :, 2, 1], BLOCK_Q, -(-WINDOW // BLOCK_Q), WINDOW - 1)

    g = jax.nn.sigmoid(gate_logits.astype(jnp.float32)).reshape(B, S, B_KV, B_REP, 3)
    o = g[..., 0:1] * o_cmp + g[..., 1:2] * o_sel + g[..., 2:3] * o_win.astype(jnp.float32)
    return o.reshape(B, S, B_WIDTH).astype(q.dtype)


def stick_breaking_attention(q, k, v):
    B, S, H, dh = q.shape
    nq = S // BLOCK_Q
    scale = dh ** -0.5
    s_pos = jnp.arange(S)
    vf = v.astype(jnp.float32)
    q_steps = q.reshape(B, nq, BLOCK_Q, H, dh).transpose(1, 0, 2, 3, 4)
    t_steps = jnp.arange(S).reshape(nq, BLOCK_Q)

    def step(args):
        qb, tb = args
        z = jnp.einsum('bqhd,bshd->bhqs', qb, k, preferred_element_type=jnp.float32) * scale
        before = s_pos[None, :] < tb[:, None]
        log_1mb = jnp.where(before, jax.nn.log_sigmoid(-z), 0.0)
        between = lax.cumsum(log_1mb, axis=3, reverse=True) - log_1mb
        a = jnp.where(before, jnp.exp(jax.nn.log_sigmoid(z) + between), 0.0)
        return jnp.einsum('bhqs,bshd->bqhd', a, vf)

    o = lax.map(step, (q_steps, t_steps))
    return o.transpose(1, 0, 2, 3, 4).reshape(B, S, H * dh).astype(q.dtype)


def hybrid_layer(x, cos, sin, g_ffn1, f1_w1, f1_w3, f1_w2, g_mix, w_in, pe_k, cw1_k, cw2_k,
                 pe_v, cw1_v, cw2_v, w_gate, w_up, w_out, g_ffn2, f2_w1, f2_w3, f2_w2):
    B, S, D = x.shape
    h = x + 0.5 * swiglu(rms_norm(x, g_ffn1), f1_w1, f1_w3, f1_w2)
    u = rms_norm(h, g_mix)
    proj = u @ w_in
    a_in = proj[..., :A_IN]
    b_in = proj[..., A_IN:A_IN + B_IN]
    c_in = proj[..., A_IN + B_IN:]

    a = a_in.reshape(B, S, 3, A_GROUPS, A_SLOTS, HEAD_DIM)
    qa = apply_rope(a[:, :, 0], cos, sin)
    ka = apply_rope(a[:, :, 1], cos, sin)
    o_a = dilated_attention(qa, ka, a[:, :, 2])

    qb = b_in[..., :B_Q_IN].reshape(B, S, B_HEADS, HEAD_DIM)
    kvb = b_in[..., B_Q_IN:B_Q_IN + B_KV_IN].reshape(B, S, 3, 2, B_KV, HEAD_DIM)
    gb = b_in[..., B_Q_IN + B_KV_IN:]
    o_b = nsa_attention(qb, kvb, gb, cos, sin, pe_k, cw1_k, cw2_k, pe_v, cw1_v, cw2_v)

    c = c_in.reshape(B, S, 3, C_HEADS, HEAD_DIM)
    o_c = stick_breaking_attention(c[:, :, 0], c[:, :, 1], c[:, :, 2])

    gates = jax.nn.sigmoid((u @ w_gate).astype(jnp.float32)).astype(u.dtype).reshape(B, S, 3, D)
    y = (gates[:, :, 0] * (o_a @ w_up[:A_WIDTH])
         + gates[:, :, 1] * (o_b @ w_up[A_WIDTH:A_WIDTH + B_WIDTH])
         + gates[:, :, 2] * (o_c @ w_up[A_WIDTH + B_WIDTH:]))
    h = h + y @ w_out
    return h + 0.5 * swiglu(rms_norm(h, g_ffn2), f2_w1, f2_w3, f2_w2)


def setup_inputs(seed: int = 0) -> dict:
    key = jax.random.key(seed)
    ks = jax.random.split(key, 24)

    def nrm(k, shape, fan_in):
        return jax.random.normal(k, shape, jnp.float32) * (fan_in ** -0.5)

    def gain(k, shape):
        return 1.0 + 0.01 * jax.random.normal(k, shape, jnp.float32)

    L = DEPTH
    x = jax.random.normal(ks[0], (BATCH, SEQ, D_MODEL), jnp.float32)
    offset = jax.random.randint(ks[1], (BATCH, 1), 0, 1024)
    positions = (offset + jnp.arange(SEQ, dtype=jnp.int32)[None, :]).astype(jnp.int32)
    return {
        'x': x,
        'positions': positions,
        'norm_ffn1': gain(ks[2], (L, D_MODEL)),
        'ffn1_w1': nrm(ks[3], (L, D_MODEL, D_FF), D_MODEL),
        'ffn1_w3': nrm(ks[4], (L, D_MODEL, D_FF), D_MODEL),
        'ffn1_w2': nrm(ks[5], (L, D_FF, D_MODEL), D_FF),
        'norm_mix': gain(ks[6], (L, D_MODEL)),
        'w_in': nrm(ks[7], (L, D_MODEL, IN_WIDTH), D_MODEL),
        'cmp_pe_k': 0.1 * jax.random.normal(ks[8], (L, CMP_BLOCK, HEAD_DIM), jnp.float32),
        'cmp_w1_k': nrm(ks[9], (L, CMP_BLOCK * HEAD_DIM, CMP_HIDDEN), CMP_BLOCK * HEAD_DIM),
        'cmp_w2_k': nrm(ks[10], (L, CMP_HIDDEN, HEAD_DIM), CMP_HIDDEN),
        'cmp_pe_v': 0.1 * jax.random.normal(ks[11], (L, CMP_BLOCK, HEAD_DIM), jnp.float32),
        'cmp_w1_v': nrm(ks[12], (L, CMP_BLOCK * HEAD_DIM, CMP_HIDDEN), CMP_BLOCK * HEAD_DIM),
        'cmp_w2_v': nrm(ks[13], (L, CMP_HIDDEN, HEAD_DIM), CMP_HIDDEN),
        'w_gate': nrm(ks[14], (L, D_MODEL, 3 * D_MODEL), D_MODEL),
        'w_up': nrm(ks[15], (L, MIX_WIDTH, D_MODEL), B_WIDTH),
        'w_out': nrm(ks[16], (L, D_MODEL, D_MODEL), D_MODEL),
        'norm_ffn2': gain(ks[17], (L, D_MODEL)),
        'ffn2_w1': nrm(ks[18], (L, D_MODEL, D_FF), D_MODEL),
        'ffn2_w3': nrm(ks[19], (L, D_MODEL, D_FF), D_MODEL),
        'ffn2_w2': nrm(ks[20], (L, D_FF, D_MODEL), D_FF),
        'norm_final': gain(ks[21], (D_MODEL,)),
    }


def reference(x, positions, norm_ffn1, ffn1_w1, ffn1_w3, ffn1_w2, norm_mix, w_in,
              cmp_pe_k, cmp_w1_k, cmp_w2_k, cmp_pe_v, cmp_w1_v, cmp_w2_v,
              w_gate, w_up, w_out, norm_ffn2, ffn2_w1, ffn2_w3, ffn2_w2, norm_final):
    cos, sin = rope_tables(positions)
    h = x
    for i in range(DEPTH):
        h = hybrid_layer(h, cos, sin, norm_ffn1[i], ffn1_w1[i], ffn1_w3[i], ffn1_w2[i],
                         norm_mix[i], w_in[i], cmp_pe_k[i], cmp_w1_k[i], cmp_w2_k[i],
                         cmp_pe_v[i], cmp_w1_v[i], cmp_w2_v[i], w_gate[i], w_up[i], w_out[i],
                         norm_ffn2[i], ffn2_w1[i], ffn2_w3[i], ffn2_w2[i])
    return rms_norm(h, norm_final)
```

```python
import functools

import numpy as np
import jax
import jax.numpy as jnp
from jax import lax
from jax.experimental import pallas as pl
from jax.experimental.pallas import tpu as pltpu

F32 = jnp.float32
BF16 = jnp.bfloat16

D_MODEL = 1024
HEAD_DIM = 64
ROT_DIM = HEAD_DIM // 4
ROPE_THETA = 500000.0
NORM_EPS = 1e-6
D_FF = 2816
LANES = 128

DIL_PAIRS = ((128, 1), (512, 4), (2048, 16))
A_GROUPS = 3
A_SLOTS = 6
A_WIDTH = A_SLOTS * HEAD_DIM
B_HEADS = 8
B_KV = 2
B_WIDTH = B_HEADS * HEAD_DIM
CMP_BLOCK = 32
CMP_STRIDE = 16
CMP_HIDDEN = 2 * HEAD_DIM
SEL_BLOCK = 64
N_SELECT = 16
WINDOW = 512
FORCE_SCORE = 1e4
C_HEADS = 6
C_WIDTH = C_HEADS * HEAD_DIM
A_IN = 3 * A_GROUPS * A_WIDTH
B_KV_IN = 3 * 2 * B_KV * HEAD_DIM
B_IN = B_WIDTH + B_KV_IN + 3 * B_HEADS
C_IN = 3 * C_WIDTH
Q_SCALE = HEAD_DIM ** -0.5

NEG = -1e30
VMEM_LIMIT = 48 * 1024 * 1024

R_WIDTH = 3072
R_AQ, R_AK, R_KSLC, R_KWIN = 4, 13, 22, 23
P_WIDTH = 3328
P_KCMP, P_VSLC, P_VWIN, P_AV, P_CQ, P_CK, P_CV = 4, 6, 7, 8, 17, 20, 23
G_WIDTH = 4608


def _cparams(sem):
    return pltpu.CompilerParams(dimension_semantics=sem, vmem_limit_bytes=VMEM_LIMIT)


def _rms(x, g):
    return x * lax.rsqrt(jnp.mean(x * x, axis=-1, keepdims=True) + NORM_EPS) * g


def _dot(a, b):
    return jnp.dot(a, b, preferred_element_type=F32)


def _dot_nt(a, b):
    return lax.dot_general(a, b, (((1,), (1,)), ((), ())), preferred_element_type=F32)


def _lane_lo():
    return lax.broadcasted_iota(jnp.int32, (1, LANES), 1) < HEAD_DIM


def _ffn_body(*refs, nf, final):
    if final:
        x_ref, g_ref, w1_ref, w3_ref, w2_ref, gf_ref, o_ref, xn_ref, acc_ref = refs
    else:
        x_ref, g_ref, w1_ref, w3_ref, w2_ref, o_ref, xn_ref, acc_ref = refs
    f = pl.program_id(1)

    @pl.when(f == 0)
    def _():
        xn_ref[...] = _rms(x_ref[...], g_ref[...]).astype(BF16)
        acc_ref[...] = jnp.zeros_like(acc_ref)

    xn = xn_ref[...]
    a = _dot(xn, w1_ref[...])
    b = _dot(xn, w3_ref[...])
    hdn = (a * jax.nn.sigmoid(a) * b).astype(BF16)
    acc_ref[...] += _dot(hdn, w2_ref[...])

    @pl.when(f == nf - 1)
    def _():
        y = x_ref[...] + 0.5 * acc_ref[...]
        if final:
            y = _rms(y, gf_ref[...])
        o_ref[...] = y


def _ffn(x, g, w1, w3, w2, g_final=None, *, tm=512, tf=1408):
    n, d = x.shape
    nf = D_FF // tf
    final = g_final is not None
    in_specs = [
        pl.BlockSpec((tm, d), lambda i, f: (i, 0)),
        pl.BlockSpec((1, d), lambda i, f: (0, 0)),
        pl.BlockSpec((d, tf), lambda i, f: (0, f)),
        pl.BlockSpec((d, tf), lambda i, f: (0, f)),
        pl.BlockSpec((tf, d), lambda i, f: (f, 0)),
    ]
    args = [x, g.reshape(1, d), w1, w3, w2]
    if final:
        in_specs.append(pl.BlockSpec((1, d), lambda i, f: (0, 0)))
        args.append(g_final.reshape(1, d))
    return pl.pallas_call(
        functools.partial(_ffn_body, nf=nf, final=final),
        grid=(n // tm, nf),
        in_specs=in_specs,
        out_specs=pl.BlockSpec((tm, d), lambda i, f: (i, 0)),
        out_shape=jax.ShapeDtypeStruct((n, d), F32),
        scratch_shapes=[pltpu.VMEM((tm, d), BF16), pltpu.VMEM((tm, d), F32)],
        compiler_params=_cparams(("parallel", "arbitrary")),
        name="ffn",
    )(*args)


def _proj_body(*refs, mode, tn):
    if mode == "rope":
        x_ref, g_ref, pos_ref, inv_ref, sa_ref, sb_ref, w_ref, o_ref, xn_ref, cos_ref, sina_ref, sinb_ref = refs
    else:
        x_ref, g_ref, w_ref, o_ref, xn_ref = refs
    j = pl.program_id(1)

    @pl.when(j == 0)
    def _():
        xn_ref[...] = _rms(x_ref[...], g_ref[...]).astype(BF16)
        if mode == "rope":
            ang = pos_ref[...] * inv_ref[...]
            cos_ref[...] = jnp.cos(ang)
            s = jnp.sin(ang)
            sina_ref[...] = s * sa_ref[...]
            sinb_ref[...] = s * sb_ref[...]

    acc = _dot(xn_ref[...], w_ref[...])
    if mode == "sigmoid":
        o_ref[...] = jax.nn.sigmoid(acc).astype(o_ref.dtype)
    elif mode == "rope":
        half = ROT_DIM // 2
        cos, sina, sinb = cos_ref[...], sina_ref[...], sinb_ref[...]
        for c in range(tn // LANES):
            y = acc[:, c * LANES:(c + 1) * LANES]
            up = pltpu.roll(y, LANES - half, axis=1)
            dn = pltpu.roll(y, half, axis=1)
            o_ref[:, c * LANES:(c + 1) * LANES] = (y * cos + up * sina + dn * sinb).astype(o_ref.dtype)
    else:
        o_ref[...] = acc.astype(o_ref.dtype)


def _proj(h, g, w, mode, rope_in=None, *, tm=1024, tn=512):
    n, d = h.shape
    cols = w.shape[1]
    in_specs = [pl.BlockSpec((tm, d), lambda i, j: (i, 0)), pl.BlockSpec((1, d), lambda i, j: (0, 0))]
    args = [h, g.reshape(1, d)]
    scratch = [pltpu.VMEM((tm, d), BF16)]
    if mode == "rope":
        posf, inv, sa, sb = rope_in
        in_specs += [pl.BlockSpec((tm, 1), lambda i, j: (i, 0))] + [pl.BlockSpec((1, LANES), lambda i, j: (0, 0))] * 3
        args += [posf, inv, sa, sb]
        scratch += [pltpu.VMEM((tm, LANES), F32)] * 3
    in_specs.append(pl.BlockSpec((d, tn), lambda i, j: (0, j)))
    args.append(w)
    return pl.pallas_call(
        functools.partial(_proj_body, mode=mode, tn=tn),
        grid=(n // tm, cols // tn),
        in_specs=in_specs,
        out_specs=pl.BlockSpec((tm, tn), lambda i, j: (i, j)),
        out_shape=jax.ShapeDtypeStruct((n, cols), BF16),
        scratch_shapes=scratch,
        compiler_params=_cparams(("parallel", "arbitrary")),
        name="proj_" + mode,
    )(*args)


def _attn_a_body(q_ref, k_ref, v_ref, o_ref, qf, kf, vf, m_s, l_s, acc_s, *, seq):
    g = pl.program_id(2)
    qf[...] = q_ref[...].astype(F32)
    kf[...] = k_ref[...].astype(F32)
    vf[...] = v_ref[...].astype(F32)
    lo = _lane_lo()
    blk = LANES
    qi = lax.broadcasted_iota(jnp.int32, (blk, 1), 0) + blk
    ki = lax.broadcasted_iota(jnp.int32, (1, 2 * blk), 1)
    dist = qi - ki

    def run_group(window, dil, first):
        steps = window // dil
        per = seq // (blk * dil)
        in_band = (dist >= 0) & (dist <= steps)

        def rows(ref, start):
            if dil == 1:
                return ref[pl.ds(pl.multiple_of(start, blk), blk), :]
            return ref[pl.ds(start, blk, stride=dil), :]

        def body(idx, carry):
            r = idx // per
            j = idx - r * per
            start = r + dil * blk * j
            prev = jnp.maximum(start - dil * blk, r)
            qb = rows(qf, start)
            kcat = jnp.concatenate([rows(kf, prev), rows(kf, start)], axis=0).astype(BF16)
            vcat = jnp.concatenate([rows(vf, prev), rows(vf, start)], axis=0).astype(BF16)
            valid = in_band & ((ki >= blk) | (j > 0))
            ms, ls, os_ = [], [], []
            for half in range(2):
                qm = jnp.where(lo if half == 0 else ~lo, qb, 0.0).astype(BF16)
                s = jnp.where(valid, _dot_nt(qm, kcat), NEG)
                m = jnp.max(s, axis=1, keepdims=True)
                p = jnp.exp(s - m)
                ls.append(jnp.sum(p, axis=1, keepdims=True))
                ms.append(m)
                os_.append(_dot(p.astype(BF16), vcat))
            m_b = jnp.where(lo, ms[0], ms[1])
            l_b = jnp.where(lo, ls[0], ls[1])
            o_b = jnp.where(lo, os_[0], os_[1])
            sl = (pl.ds(pl.multiple_of(start, blk), blk) if dil == 1 else pl.ds(start, blk, stride=dil), slice(None))
            if first:
                m_s[sl] = m_b
                l_s[sl] = l_b
                acc_s[sl] = o_b
            else:
                m_old = m_s[sl]
                m_new = jnp.maximum(m_old, m_b)
                a_old = jnp.exp(m_old - m_new)
                a_b = jnp.exp(m_b - m_new)
                m_s[sl] = m_new
                l_s[sl] = l_s[sl] * a_old + l_b * a_b
                acc_s[sl] = acc_s[sl] * a_old + o_b * a_b
            return carry

        lax.fori_loop(0, seq // blk, body, 0)

    for gi, (window, dil) in enumerate(DIL_PAIRS):
        pl.when(g == gi)(functools.partial(run_group, window, dil, gi == 0))

    @pl.when(g == A_GROUPS - 1)
    def _():
        o_ref[...] = (acc_s[...] / l_s[...]).astype(o_ref.dtype)


def _attn_a(r3, p3):
    bsz, seq, _ = r3.shape
    npair = A_SLOTS // 2
    blk = lambda off: pl.BlockSpec((None, seq, LANES), lambda b, p, g: (b, 0, off + g * npair + p))
    return pl.pallas_call(
        functools.partial(_attn_a_body, seq=seq),
        grid=(bsz, npair, A_GROUPS),
        in_specs=[blk(R_AQ), blk(R_AK), blk(P_AV)],
        out_specs=pl.BlockSpec((None, seq, LANES), lambda b, p, g: (b, 0, p)),
        out_shape=jax.ShapeDtypeStruct((bsz, seq, A_WIDTH), BF16),
        scratch_shapes=[pltpu.VMEM((seq, LANES), F32)] * 6,
        compiler_params=_cparams(("parallel", "parallel", "arbitrary")),
        name="attn_dilated",
    )(r3, r3, p3)


def _compress_body(x_ref, pe_ref, wlo_ref, whi_ref, w2_ref, o_ref, *, nc):
    x = x_ref[...].astype(F32)
    ylo = _dot((x + pe_ref[0:1, :]).astype(BF16), wlo_ref[...])
    yhi = _dot((x + pe_ref[1:2, :]).astype(BF16), whi_ref[...])
    pre = ylo + pltpu.roll(yhi, nc - 1, axis=0)
    o_ref[...] = _dot(jax.nn.gelu(pre).astype(BF16), w2_ref[...]).astype(o_ref.dtype)


def _compress(xc, pe, wlo, whi, w2):
    bsz, _, nc, width = xc.shape
    hid = wlo.shape[-1]
    return pl.pallas_call(
        functools.partial(_compress_body, nc=nc),
        grid=(bsz, 2),
        in_specs=[
            pl.BlockSpec((None, None, nc, width), lambda b, t: (b, t, 0, 0)),
            pl.BlockSpec((None, 2, width), lambda b, t: (t, 0, 0)),
            pl.BlockSpec((None, width, hid), lambda b, t: (t, 0, 0)),
            pl.BlockSpec((None, width, hid), lambda b, t: (t, 0, 0)),
            pl.BlockSpec((None, hid, LANES), lambda b, t: (t, 0, 0)),
        ],
        out_specs=pl.BlockSpec((None, None, nc, LANES), lambda b, t: (b, t, 0, 0)),
        out_shape=jax.ShapeDtypeStruct((bsz, 2, nc, LANES), BF16),
        compiler_params=_cparams(("parallel", "parallel")),
        name="nsa_compress",
    )(xc, pe, wlo, whi, w2)


def _split3(x):
    hi = x.astype(BF16)
    r1 = x - hi.astype(F32)
    mid = r1.astype(BF16)
    lo = (r1 - mid.astype(F32)).astype(BF16)
    return hi, mid, lo


def _cmp_body(q_ref, kcv_ref, mm_ref, gate_ref, ocmp_ref, sel_ref, *, tq, nc):
    t0 = pl.program_id(1) * tq
    q = q_ref[...]
    kc = kcv_ref[0]
    vc = kcv_ref[1]
    lo = _lane_lo()
    t = t0 + lax.broadcasted_iota(jnp.int32, (tq, 1), 0)
    cend = lax.broadcasted_iota(jnp.int32, (1, nc), 1) * CMP_STRIDE + (CMP_BLOCK - 1)
    valid = cend <= t
    imp = [jnp.zeros((tq, nc), F32), jnp.zeros((tq, nc), F32)]
    for c in range(B_HEADS // 2):
        qc = q[:, c * LANES:(c + 1) * LANES]
        outs = []
        for half in range(2):
            qm = jnp.where(lo if half == 0 else ~lo, qc, jnp.zeros_like(qc))
            s = jnp.where(valid, _dot_nt(qm, kc), NEG)
            m = jnp.max(s, axis=1, keepdims=True)
            p = jnp.where(valid, jnp.exp(s - m), 0.0)
            l = jnp.sum(p, axis=1, keepdims=True)
            pn = p * (1.0 / jnp.where(l > 0, l, 1.0))
            imp[half] = imp[half] + pn
            outs.append(_dot(pn.astype(BF16), vc))
        oc = jnp.where(lo, outs[0], outs[1])
        ocmp_ref[:, c * LANES:(c + 1) * LANES] = oc * gate_ref[:, c * LANES:(c + 1) * LANES].astype(F32)

    score = jnp.zeros((tq, LANES), F32)
    for half in range(2):
        for part in _split3(imp[half]):
            score = score + _dot(part, mm_ref[half])
    lane = lax.broadcasted_iota(jnp.int32, (1, LANES), 1)
    jb = jnp.where(lane < SEL_BLOCK, lane, lane - SEL_BLOCK)
    cur = t // SEL_BLOCK
    forced = (jb == 0) | (jb == cur) | (jb == cur - 1)
    score = jnp.where(forced, FORCE_SCORE, jnp.where(jb * SEL_BLOCK <= t, score, -1.0))
    rank = jnp.zeros((tq, LANES), F32)
    for jp in range(SEL_BLOCK):
        col = jnp.where(lo, score[:, jp:jp + 1], score[:, SEL_BLOCK + jp:SEL_BLOCK + jp + 1])
        beats = (col > score) | ((col == score) & (jp < jb))
        rank = rank + jnp.where(beats, 1.0, 0.0)
    sel_ref[...] = jnp.where(rank < N_SELECT, 1.0, 0.0).astype(sel_ref.dtype)


def _cmp_select(p3, kcv, mm, g3, *, tq=128):
    bsz, seq, _ = p3.shape
    nc = kcv.shape[2]
    return pl.pallas_call(
        functools.partial(_cmp_body, tq=tq, nc=nc),
        grid=(bsz, seq // tq),
        in_specs=[
            pl.BlockSpec((None, tq, B_WIDTH), lambda b, i: (b, i, 0)),
            pl.BlockSpec((None, 2, nc, LANES), lambda b, i: (b, 0, 0, 0)),
            pl.BlockSpec((2, nc, LANES), lambda b, i: (0, 0, 0)),
            pl.BlockSpec((None, tq, B_WIDTH), lambda b, i: (b, i, 6)),
        ],
        out_specs=[
            pl.BlockSpec((None, tq, B_WIDTH), lambda b, i: (b, i, 0)),
            pl.BlockSpec((None, tq, LANES), lambda b, i: (b, i, 0)),
        ],
        out_shape=[
            jax.ShapeDtypeStruct((bsz, seq, B_WIDTH), F32),
            jax.ShapeDtypeStruct((bsz, seq, LANES), BF16),
        ],
        compiler_params=_cparams(("parallel", "parallel")),
        name="nsa_cmp_select",
    )(p3, kcv, mm, g3)


def _selwin_body(q_ref, ks_ref, vs_ref, kw_ref, vw_ref, sel_ref, ocmp_ref, gs_ref, gw_ref, o_ref, *, tq, tk):
    t0 = pl.program_id(1) * tq
    nh = B_HEADS // 2
    rows = nh * tq
    qall = q_ref[...]
    selq = sel_ref[...]
    lo = _lane_lo()
    trow = t0 + lax.broadcasted_iota(jnp.int32, (tq, 1), 0)
    trow = jnp.concatenate([trow] * nh, axis=0)
    span = WINDOW + tq
    wstart = pl.multiple_of(jnp.maximum(t0 - WINDOW, 0), tq)
    kposw = wstart + lax.broadcasted_iota(jnp.int32, (1, span), 1)
    dw = trow - kposw
    wvalid = (dw >= 0) & (dw < WINDOW)
    nkt = (t0 + tq + tk - 1) // tk
    erow = lax.broadcasted_iota(jnp.int32, (LANES, tk), 0)
    ecol = lax.broadcasted_iota(jnp.int32, (LANES, tk), 1) // SEL_BLOCK
    o_sel, o_win = [], []
    for half in range(2):
        hm = lo if half == 0 else ~lo
        qs = jnp.concatenate(
            [jnp.where(hm, qall[:, c * LANES:(c + 1) * LANES], jnp.zeros((tq, LANES), BF16)) for c in range(nh)], axis=0)

        kw = kw_ref[pl.ds(wstart, span), :]
        vw = vw_ref[pl.ds(wstart, span), :]
        s = jnp.where(wvalid, _dot_nt(qs, kw), NEG)
        m = jnp.max(s, axis=1, keepdims=True)
        p = jnp.exp(s - m)
        l = jnp.sum(p, axis=1, keepdims=True)
        o_win.append(_dot(p.astype(BF16), vw) * (1.0 / l))

        def kt_body(kt, carry):
            m_i, l_i, acc = carry
            koff = pl.multiple_of(kt * tk, tk)
            ks = ks_ref[pl.ds(koff, tk), :]
            vs = vs_ref[pl.ds(koff, tk), :]
            expand = jnp.where(erow == half * SEL_BLOCK + kt * (tk // SEL_BLOCK) + ecol, 1.0, 0.0).astype(BF16)
            chosen = _dot(selq, expand)
            chosen = jnp.concatenate([chosen] * nh, axis=0)
            kpos = koff + lax.broadcasted_iota(jnp.int32, (1, tk), 1)
            ok = (chosen > 0.5) & (kpos <= trow)
            s = jnp.where(ok, _dot_nt(qs, ks), NEG)
            m_new = jnp.maximum(m_i, jnp.max(s, axis=1, keepdims=True))
            alpha = jnp.exp(m_i - m_new)
            p = jnp.where(ok, jnp.exp(s - m_new), 0.0)
            l_new = alpha * l_i + jnp.sum(p, axis=1, keepdims=True)
            acc = alpha * acc + _dot(p.astype(BF16), vs)
            return m_new, l_new, acc

        init = (jnp.full((rows, 1), NEG, F32), jnp.zeros((rows, 1), F32), jnp.zeros((rows, LANES), F32))
        _, l_f, acc_f = lax.fori_loop(0, nkt, kt_body, init)
        o_sel.append(acc_f * (1.0 / l_f))

    for c in range(nh):
        rs = slice(c * tq, (c + 1) * tq)
        cs = slice(c * LANES, (c + 1) * LANES)
        osel = jnp.where(lo, o_sel[0][rs], o_sel[1][rs])
        owin = jnp.where(lo, o_win[0][rs], o_win[1][rs])
        o_ref[:, cs] = (ocmp_ref[:, cs] + gs_ref[:, cs].astype(F32) * osel
                        + gw_ref[:, cs].astype(F32) * owin).astype(o_ref.dtype)


def _sel_win(r3, p3, sel, ocmp, g3, *, tq=128, tk=512):
    bsz, seq, _ = r3.shape
    full = lambda off: pl.BlockSpec((None, seq, LANES), lambda b, i: (b, 0, off))
    tile = lambda off: pl.BlockSpec((None, tq, B_WIDTH), lambda b, i: (b, i, off))
    return pl.pallas_call(
        functools.partial(_selwin_body, tq=tq, tk=tk),
        grid=(bsz, seq // tq),
        in_specs=[tile(0), full(R_KSLC), full(P_VSLC), full(R_KWIN), full(P_VWIN),
                  pl.BlockSpec((None, tq, LANES), lambda b, i: (b, i, 0)), tile(0), tile(7), tile(8)],
        out_specs=tile(0),
        out_shape=jax.ShapeDtypeStruct((bsz, seq, B_WIDTH), BF16),
        compiler_params=_cparams(("parallel", "parallel")),
        name="nsa_sel_win",
    )(r3, r3, p3, r3, p3, sel, ocmp, g3, g3)


def _stick_body(q_ref, k_ref, v_ref, o_ref, carry_ref, acc_ref, *, tq, tk):
    i = pl.program_id(2)
    t0 = i * tq
    q = q_ref[...]
    lo = _lane_lo()
    qh = [jnp.where(lo, q, jnp.zeros_like(q)), jnp.where(lo, jnp.zeros_like(q), q)]
    tpos = t0 + lax.broadcasted_iota(jnp.int32, (tq, 1), 0)
    later = jnp.where(lax.broadcasted_iota(jnp.int32, (tk, tk), 0) > lax.broadcasted_iota(jnp.int32, (tk, tk), 1),
                      1.0, 0.0).astype(BF16)
    carry_ref[...] = jnp.zeros_like(carry_ref)
    acc_ref[...] = jnp.zeros_like(acc_ref)
    nkb = (i + 1) * (tq // tk)

    def body(n, c):
        koff = pl.multiple_of((nkb - 1 - n) * tk, tk)
        k = k_ref[pl.ds(koff, tk), :]
        v = v_ref[pl.ds(koff, tk), :]
        before = (koff + lax.broadcasted_iota(jnp.int32, (1, tk), 1)) < tpos
        for half in range(2):
            z = _dot_nt(qh[half], k)
            sp = jnp.maximum(z, 0.0) + jnp.log1p(jnp.exp(-jnp.abs(z)))
            lg = jnp.where(before, -sp, 0.0)
            hi = lg.astype(BF16)
            lw = (lg - hi.astype(F32)).astype(BF16)
            between = carry_ref[half] + _dot(hi, later) + _dot(lw, later)
            a = jnp.where(before, jnp.exp(z - sp + between), 0.0)
            acc_ref[half] += _dot(a.astype(BF16), v)
            carry_ref[half] += jnp.sum(lg, axis=1, keepdims=True)
        return c

    lax.fori_loop(0, nkb, body, 0)
    o_ref[...] = jnp.where(lo, acc_ref[0], acc_ref[1]).astype(o_ref.dtype)


def _stick(p3, *, tq=512, tk=128):
    bsz, seq, _ = p3.shape
    npair = C_HEADS // 2
    full = lambda off: pl.BlockSpec((None, seq, LANES), lambda b, p, i: (b, 0, off + p))
    return pl.pallas_call(
        functools.partial(_stick_body, tq=tq, tk=tk),
        grid=(bsz, npair, seq // tq),
        in_specs=[pl.BlockSpec((None, tq, LANES), lambda b, p, i: (b, i, P_CQ + p)), full(P_CK), full(P_CV)],
        out_specs=pl.BlockSpec((None, tq, LANES), lambda b, p, i: (b, i, p)),
        out_shape=jax.ShapeDtypeStruct((bsz, seq, C_WIDTH), BF16),
        scratch_shapes=[pltpu.VMEM((2, tq, LANES), F32)] * 2,
        compiler_params=_cparams(("parallel", "parallel", "parallel")),
        name="stick_breaking",
    )(p3, p3, p3)


def _merge_body(h_ref, oa_ref, ob_ref, oc_ref, ga_ref, gb_ref, gc_ref, wa_ref, wb_ref, wc_ref, wo_ref, o_ref):
    y = (ga_ref[...].astype(F32) * _dot(oa_ref[...], wa_ref[...])
         + gb_ref[...].astype(F32) * _dot(ob_ref[...], wb_ref[...])
         + gc_ref[...].astype(F32) * _dot(oc_ref[...], wc_ref[...]))
    o_ref[...] = h_ref[...] + _dot(y.astype(BF16), wo_ref[...])


def _merge(h, oa, ob, oc, gates, wa, wb, wc, wo, *, tm=512):
    n, d = h.shape
    row = lambda w: pl.BlockSpec((tm, w), lambda i: (i, 0))
    whole = lambda a: pl.BlockSpec(a.shape, lambda i: (0, 0))
    gate = lambda m: pl.BlockSpec((tm, d), lambda i: (i, m))
    return pl.pallas_call(
        _merge_body,
        grid=(n // tm,),
        in_specs=[row(d), row(A_WIDTH), row(B_WIDTH), row(C_WIDTH), gate(0), gate(1), gate(2),
                  whole(wa), whole(wb), whole(wc), whole(wo)],
        out_specs=row(d),
        out_shape=jax.ShapeDtypeStruct((n, d), F32),
        compiler_params=_cparams(("parallel",)),
        name="merge_out",
    )(h, oa, ob, oc, gates, gates, gates, wa, wb, wc, wo)


_B_HEAD_ORDER = np.array([0, 4, 1, 5, 2, 6, 3, 7])


def _b_cols(base):
    return (base + (_B_HEAD_ORDER[:, None] * HEAD_DIM + np.arange(HEAD_DIM)[None, :])).reshape(-1)


def _prep_layer(w_in, w_gate, w_up, pe_k, cw1_k, cw2_k, pe_v, cw1_v, cw2_v):
    b0 = A_IN
    kv0 = b0 + B_WIDTH
    gt0 = kv0 + B_KV_IN
    c0 = b0 + B_IN
    kvcol = lambda br, kvt: kv0 + (br * 2 + kvt) * B_KV * HEAD_DIM
    span = lambda s, w: np.arange(s, s + w)
    wq_b = w_in[:, _b_cols(b0)] * Q_SCALE
    w_rope = jnp.concatenate([
        wq_b,
        w_in[:, span(0, A_GROUPS * A_WIDTH)] * Q_SCALE,
        w_in[:, span(A_GROUPS * A_WIDTH, A_GROUPS * A_WIDTH)],
        w_in[:, span(kvcol(1, 0), LANES)],
        w_in[:, span(kvcol(2, 0), LANES)],
    ], axis=1).astype(BF16)
    wc = w_in[:, c0:]
    w_plain = jnp.concatenate([
        wq_b,
        w_in[:, span(kvcol(0, 0), LANES)], w_in[:, span(kvcol(0, 1), LANES)],
        w_in[:, span(kvcol(1, 1), LANES)], w_in[:, span(kvcol(2, 1), LANES)],
        w_in[:, span(2 * A_GROUPS * A_WIDTH, A_GROUPS * A_WIDTH)],
        wc[:, :C_WIDTH] * Q_SCALE, wc[:, C_WIDTH:],
    ], axis=1).astype(BF16)
    gcols = [gt0 + 3 * np.repeat(_B_HEAD_ORDER, HEAD_DIM) + j for j in range(3)]
    w_sig = jnp.concatenate([w_gate] + [w_in[:, gc] for gc in gcols], axis=1).astype(BF16)

    def cmp_weights(pe, w1, w2):
        w1r = w1.reshape(2, CMP_STRIDE, HEAD_DIM, CMP_HIDDEN)
        z = jnp.zeros_like(w1r)
        per_head = [jnp.stack([w1r, z], axis=2), jnp.stack([z, w1r], axis=2)]
        wfull = jnp.concatenate(per_head, axis=-1).reshape(2, CMP_STRIDE * LANES, 2 * CMP_HIDDEN)
        zz = jnp.zeros_like(w2)
        w2f = jnp.concatenate([jnp.concatenate([w2, zz], axis=1), jnp.concatenate([zz, w2], axis=1)], axis=0)
        per = pe.reshape(2, CMP_STRIDE, 1, HEAD_DIM)
        pef = jnp.broadcast_to(per, (2, CMP_STRIDE, B_KV, HEAD_DIM)).reshape(2, CMP_STRIDE * LANES)
        return pef, wfull[0].astype(BF16), wfull[1].astype(BF16), w2f.astype(BF16)

    ck, cv = cmp_weights(pe_k, cw1_k, cw2_k), cmp_weights(pe_v, cw1_v, cw2_v)
    cmp_w = tuple(jnp.stack([a, b]) for a, b in zip(ck, cv))
    wb_rows = A_WIDTH + _b_cols(0)
    w_up = w_up.astype(BF16)
    return dict(w_rope=w_rope, w_plain=w_plain, w_sig=w_sig, cmp_w=cmp_w,
                wa=w_up[:A_WIDTH], wb=w_up[wb_rows], wc=w_up[A_WIDTH + B_WIDTH:])


def _rope_consts():
    inv = ROPE_THETA ** (-jnp.arange(0, ROT_DIM, 2, dtype=F32) / ROT_DIM)
    lane = np.arange(LANES) % HEAD_DIM
    half = ROT_DIM // 2
    inv_l = jnp.where(lane < ROT_DIM, inv[lane % half], 0.0).reshape(1, LANES).astype(F32)
    sa = np.where(lane < half, -1.0, 0.0).reshape(1, LANES).astype(np.float32)
    sb = np.where((lane >= half) & (lane < ROT_DIM), 1.0, 0.0).reshape(1, LANES).astype(np.float32)
    return inv_l, jnp.asarray(sa), jnp.asarray(sb)


def _block_sum_matrix(nc):
    ratio = SEL_BLOCK // CMP_STRIDE
    per = CMP_BLOCK // CMP_STRIDE
    m = np.zeros((2, nc, LANES), np.float32)
    for j in range(SEL_BLOCK):
        for a in range(ratio):
            for b in range(per):
                c = ratio * j + a + b
                if c < nc - 1:
                    m[0, c, j] += 1.0
                    m[1, c, SEL_BLOCK + j] += 1.0
    return jnp.asarray(m).astype(BF16)


def _mixers(h, lw, norm_mix, rope_in, bsz, seq):
    n = bsz * seq
    r = _proj(h, norm_mix, lw["w_rope"], "rope", rope_in)
    p = _proj(h, norm_mix, lw["w_plain"], "plain", tn=P_WIDTH // 2)
    g = _proj(h, norm_mix, lw["w_sig"], "sigmoid")
    r3, p3, g3 = r.reshape(bsz, seq, R_WIDTH), p.reshape(bsz, seq, P_WIDTH), g.reshape(bsz, seq, G_WIDTH)
    oa = _attn_a(r3, p3)
    nc = seq // CMP_STRIDE
    xc = p3[:, :, P_KCMP * LANES:(P_KCMP + 2) * LANES].reshape(bsz, nc, CMP_STRIDE, 2, LANES)
    xc = xc.transpose(0, 3, 1, 2, 4).reshape(bsz, 2, nc, CMP_STRIDE * LANES)
    kcv = _compress(xc, *lw["cmp_w"])
    ocmp, sel = _cmp_select(p3, kcv, _block_sum_matrix(nc), g3)
    ob = _sel_win(r3, p3, sel, ocmp, g3)
    oc = _stick(p3)
    return oa.reshape(n, A_WIDTH), ob.reshape(n, B_WIDTH), oc.reshape(n, C_WIDTH), g


def kernel(x, positions, norm_ffn1, ffn1_w1, ffn1_w3, ffn1_w2, norm_mix, w_in, cmp_pe_k, cmp_w1_k, cmp_w2_k,
           cmp_pe_v, cmp_w1_v, cmp_w2_v, w_gate, w_up, w_out, norm_ffn2, ffn2_w1, ffn2_w3, ffn2_w2, norm_final):
    bsz, seq, d = x.shape
    n = bsz * seq
    depth = w_in.shape[0]
    rope_in = (positions.astype(F32).reshape(n, 1),) + _rope_consts()
    h = x.reshape(n, d)
    for i in range(depth):
        lw = _prep_layer(w_in[i], w_gate[i], w_up[i], cmp_pe_k[i], cmp_w1_k[i], cmp_w2_k[i],
                         cmp_pe_v[i], cmp_w1_v[i], cmp_w2_v[i])
        h = _ffn(h, norm_ffn1[i], ffn1_w1[i].astype(BF16), ffn1_w3[i].astype(BF16), ffn1_w2[i].astype(BF16))
        oa, ob, oc, g = _mixers(h, lw, norm_mix[i], rope_in, bsz, seq)
        h = _merge(h, oa, ob, oc, g, lw["wa"], lw["wb"], lw["wc"], w_out[i].astype(BF16))
        h = _ffn(h, norm_ffn2[i], ffn2_w1[i].astype(BF16), ffn2_w3[i].astype(BF16), ffn2_w2[i].astype(BF16),
                 norm_final if i == depth - 1 else None)
    return h.reshape(bsz, seq, d)
```

```python
import functools

import numpy as np
import jax
import jax.numpy as jnp
from jax import lax
from jax.experimental import pallas as pl
from jax.experimental.pallas import tpu as pltpu

F32 = jnp.float32
BF16 = jnp.bfloat16

D_MODEL = 1024
HEAD_DIM = 64
ROT_DIM = HEAD_DIM // 4
ROPE_THETA = 500000.0
NORM_EPS = 1e-6
D_FF = 2816
LANES = 128

DIL_PAIRS = ((128, 1), (512, 4), (2048, 16))
A_GROUPS = 3
A_SLOTS = 6
A_WIDTH = A_SLOTS * HEAD_DIM
B_HEADS = 8
B_KV = 2
B_WIDTH = B_HEADS * HEAD_DIM
CMP_BLOCK = 32
CMP_STRIDE = 16
CMP_HIDDEN = 2 * HEAD_DIM
SEL_BLOCK = 64
N_SELECT = 16
WINDOW = 512
FORCE_SCORE = 1e4
C_HEADS = 6
C_WIDTH = C_HEADS * HEAD_DIM
A_IN = 3 * A_GROUPS * A_WIDTH
B_KV_IN = 3 * 2 * B_KV * HEAD_DIM
B_IN = B_WIDTH + B_KV_IN + 3 * B_HEADS
C_IN = 3 * C_WIDTH
Q_SCALE = HEAD_DIM ** -0.5

NEG = -1e30
VMEM_LIMIT = 48 * 1024 * 1024

R_WIDTH = 3072
R_AQ, R_AK, R_KSLC, R_KWIN = 4, 13, 22, 23
P_WIDTH = 3328
P_KCMP, P_VSLC, P_VWIN, P_AV, P_CQ, P_CK, P_CV = 4, 6, 7, 8, 17, 20, 23
G_WIDTH = 4608


def _cparams(sem):
    return pltpu.CompilerParams(dimension_semantics=sem, vmem_limit_bytes=VMEM_LIMIT)


def _rms(x, g):
    return x * lax.rsqrt(jnp.mean(x * x, axis=-1, keepdims=True) + NORM_EPS) * g


def _dot(a, b):
    return jnp.dot(a, b, preferred_element_type=F32)


def _dot_nt(a, b):
    return lax.dot_general(a, b, (((1,), (1,)), ((), ())), preferred_element_type=F32)


def _lane_lo():
    return lax.broadcasted_iota(jnp.int32, (1, LANES), 1) < HEAD_DIM


def _ffn_body(*refs, nf, final):
    if final:
        x_ref, g_ref, w1_ref, w3_ref, w2_ref, gf_ref, o_ref, xn_ref, acc_ref = refs
    else:
        x_ref, g_ref, w1_ref, w3_ref, w2_ref, o_ref, xn_ref, acc_ref = refs
    f = pl.program_id(1)

    @pl.when(f == 0)
    def _():
        xn_ref[...] = _rms(x_ref[...], g_ref[...]).astype(BF16)
        acc_ref[...] = jnp.zeros_like(acc_ref)

    xn = xn_ref[...]
    a = _dot(xn, w1_ref[...])
    b = _dot(xn, w3_ref[...])
    hdn = (a * jax.nn.sigmoid(a) * b).astype(BF16)
    acc_ref[...] += _dot(hdn, w2_ref[...])

    @pl.when(f == nf - 1)
    def _():
        y = x_ref[...] + 0.5 * acc_ref[...]
        if final:
            y = _rms(y, gf_ref[...])
        o_ref[...] = y


def _ffn(x, g, w1, w3, w2, g_final=None, *, tm=512, tf=1408):
    n, d = x.shape
    nf = D_FF // tf
    final = g_final is not None
    in_specs = [
        pl.BlockSpec((tm, d), lambda i, f: (i, 0)),
        pl.BlockSpec((1, d), lambda i, f: (0, 0)),
        pl.BlockSpec((d, tf), lambda i, f: (0, f)),
        pl.BlockSpec((d, tf), lambda i, f: (0, f)),
        pl.BlockSpec((tf, d), lambda i, f: (f, 0)),
    ]
    args = [x, g.reshape(1, d), w1, w3, w2]
    if final:
        in_specs.append(pl.BlockSpec((1, d), lambda i, f: (0, 0)))
        args.append(g_final.reshape(1, d))
    return pl.pallas_call(
        functools.partial(_ffn_body, nf=nf, final=final),
        grid=(n // tm, nf),
        in_specs=in_specs,
        out_specs=pl.BlockSpec((tm, d), lambda i, f: (i, 0)),
        out_shape=jax.ShapeDtypeStruct((n, d), F32),
        scratch_shapes=[pltpu.VMEM((tm, d), BF16), pltpu.VMEM((tm, d), F32)],
        compiler_params=_cparams(("parallel", "arbitrary")),
        name="ffn",
    )(*args)


def _proj_body(*refs, mode, tn):
    if mode == "rope":
        x_ref, g_ref, pos_ref, inv_ref, sa_ref, sb_ref, w_ref, o_ref, xn_ref, cos_ref, sina_ref, sinb_ref = refs
    else:
        x_ref, g_ref, w_ref, o_ref, xn_ref = refs
    j = pl.program_id(1)

    @pl.when(j == 0)
    def _():
        xn_ref[...] = _rms(x_ref[...], g_ref[...]).astype(BF16)
        if mode == "rope":
            ang = pos_ref[...] * inv_ref[...]
            cos_ref[...] = jnp.cos(ang)
            s = jnp.sin(ang)
            sina_ref[...] = s * sa_ref[...]
            sinb_ref[...] = s * sb_ref[...]

    acc = _dot(xn_ref[...], w_ref[...])
    if mode == "sigmoid":
        o_ref[...] = jax.nn.sigmoid(acc).astype(o_ref.dtype)
    elif mode == "rope":
        half = ROT_DIM // 2
        cos, sina, sinb = cos_ref[...], sina_ref[...], sinb_ref[...]
        for c in range(tn // LANES):
            y = acc[:, c * LANES:(c + 1) * LANES]
            up = pltpu.roll(y, LANES - half, axis=1)
            dn = pltpu.roll(y, half, axis=1)
            o_ref[:, c * LANES:(c + 1) * LANES] = (y * cos + up * sina + dn * sinb).astype(o_ref.dtype)
    else:
        o_ref[...] = acc.astype(o_ref.dtype)


def _proj(h, g, w, mode, rope_in=None, *, tm=1024, tn=512):
    n, d = h.shape
    cols = w.shape[1]
    in_specs = [pl.BlockSpec((tm, d), lambda i, j: (i, 0)), pl.BlockSpec((1, d), lambda i, j: (0, 0))]
    args = [h, g.reshape(1, d)]
    scratch = [pltpu.VMEM((tm, d), BF16)]
    if mode == "rope":
        posf, inv, sa, sb = rope_in
        in_specs += [pl.BlockSpec((tm, 1), lambda i, j: (i, 0))] + [pl.BlockSpec((1, LANES), lambda i, j: (0, 0))] * 3
        args += [posf, inv, sa, sb]
        scratch += [pltpu.VMEM((tm, LANES), F32)] * 3
    in_specs.append(pl.BlockSpec((d, tn), lambda i, j: (0, j)))
    args.append(w)
    return pl.pallas_call(
        functools.partial(_proj_body, mode=mode, tn=tn),
        grid=(n // tm, cols // tn),
        in_specs=in_specs,
        out_specs=pl.BlockSpec((tm, tn), lambda i, j: (i, j)),
        out_shape=jax.ShapeDtypeStruct((n, cols), BF16),
        scratch_shapes=scratch,
        compiler_params=_cparams(("parallel", "arbitrary")),
        name="proj_" + mode,
    )(*args)


def _attn_a_body(q_ref, k_ref, v_ref, o_ref, qf, kf, vf, m_s, l_s, acc_s, *, seq):
    g = pl.program_id(2)
    qf[...] = q_ref[...].astype(F32)
    kf[...] = k_ref[...].astype(F32)
    vf[...] = v_ref[...].astype(F32)
    lo = _lane_lo()
    blk = LANES
    qi = lax.broadcasted_iota(jnp.int32, (blk, 1), 0) + blk
    ki = lax.broadcasted_iota(jnp.int32, (1, 2 * blk), 1)
    dist = qi - ki

    def run_group(window, dil, first):
        steps = window // dil
        per = seq // (blk * dil)
        in_band = (dist >= 0) & (dist <= steps)

        def rows(ref, start):
            if dil == 1:
                return ref[pl.ds(pl.multiple_of(start, blk), blk), :]
            return ref[pl.ds(start, blk, stride=dil), :]

        def body(idx, carry):
            r = idx // per
            j = idx - r * per
            start = r + dil * blk * j
            prev = jnp.maximum(start - dil * blk, r)
            qb = rows(qf, start)
            kcat = jnp.concatenate([rows(kf, prev), rows(kf, start)], axis=0).astype(BF16)
            vcat = jnp.concatenate([rows(vf, prev), rows(vf, start)], axis=0).astype(BF16)
            valid = in_band & ((ki >= blk) | (j > 0))
            ms, ls, os_ = [], [], []
            for half in range(2):
                qm = jnp.where(lo if half == 0 else ~lo, qb, 0.0).astype(BF16)
                s = jnp.where(valid, _dot_nt(qm, kcat), NEG)
                m = jnp.max(s, axis=1, keepdims=True)
                p = jnp.exp(s - m)
                ls.append(jnp.sum(p, axis=1, keepdims=True))
                ms.append(m)
                os_.append(_dot(p.astype(BF16), vcat))
            m_b = jnp.where(lo, ms[0], ms[1])
            l_b = jnp.where(lo, ls[0], ls[1])
            o_b = jnp.where(lo, os_[0], os_[1])
            sl = (pl.ds(pl.multiple_of(start, blk), blk) if dil == 1 else pl.ds(start, blk, stride=dil), slice(None))
            if first:
                m_s[sl] = m_b
                l_s[sl] = l_b
                acc_s[sl] = o_b
            else:
                m_old = m_s[sl]
                m_new = jnp.maximum(m_old, m_b)
                a_old = jnp.exp(m_old - m_new)
                a_b = jnp.exp(m_b - m_new)
                m_s[sl] = m_new
                l_s[sl] = l_s[sl] * a_old + l_b * a_b
                acc_s[sl] = acc_s[sl] * a_old + o_b * a_b
            return carry

        lax.fori_loop(0, seq // blk, body, 0, unroll=4)

    for gi, (window, dil) in enumerate(DIL_PAIRS):
        pl.when(g == gi)(functools.partial(run_group, window, dil, gi == 0))

    @pl.when(g == A_GROUPS - 1)
    def _():
        o_ref[...] = (acc_s[...] / l_s[...]).astype(o_ref.dtype)


def _attn_a(r3, p3):
    bsz, seq, _ = r3.shape
    npair = A_SLOTS // 2
    blk = lambda off: pl.BlockSpec((None, seq, LANES), lambda b, p, g: (b, 0, off + g * npair + p))
    return pl.pallas_call(
        functools.partial(_attn_a_body, seq=seq),
        grid=(bsz, npair, A_GROUPS),
        in_specs=[blk(R_AQ), blk(R_AK), blk(P_AV)],
        out_specs=pl.BlockSpec((None, seq, LANES), lambda b, p, g: (b, 0, p)),
        out_shape=jax.ShapeDtypeStruct((bsz, seq, A_WIDTH), BF16),
        scratch_shapes=[pltpu.VMEM((seq, LANES), F32)] * 6,
        compiler_params=_cparams(("parallel", "parallel", "arbitrary")),
        name="attn_dilated",
    )(r3, r3, p3)


def _compress_body(x_ref, pe_ref, wlo_ref, whi_ref, w2_ref, o_ref, *, nc):
    x = x_ref[...].astype(F32)
    ylo = _dot((x + pe_ref[0:1, :]).astype(BF16), wlo_ref[...])
    yhi = _dot((x + pe_ref[1:2, :]).astype(BF16), whi_ref[...])
    pre = ylo + pltpu.roll(yhi, nc - 1, axis=0)
    o_ref[...] = _dot(jax.nn.gelu(pre).astype(BF16), w2_ref[...]).astype(o_ref.dtype)


def _compress(xc, pe, wlo, whi, w2):
    bsz, _, nc, width = xc.shape
    hid = wlo.shape[-1]
    return pl.pallas_call(
        functools.partial(_compress_body, nc=nc),
        grid=(bsz, 2),
        in_specs=[
            pl.BlockSpec((None, None, nc, width), lambda b, t: (b, t, 0, 0)),
            pl.BlockSpec((None, 2, width), lambda b, t: (t, 0, 0)),
            pl.BlockSpec((None, width, hid), lambda b, t: (t, 0, 0)),
            pl.BlockSpec((None, width, hid), lambda b, t: (t, 0, 0)),
            pl.BlockSpec((None, hid, LANES), lambda b, t: (t, 0, 0)),
        ],
        out_specs=pl.BlockSpec((None, None, nc, LANES), lambda b, t: (b, t, 0, 0)),
        out_shape=jax.ShapeDtypeStruct((bsz, 2, nc, LANES), BF16),
        compiler_params=_cparams(("parallel", "parallel")),
        name="nsa_compress",
    )(xc, pe, wlo, whi, w2)


def _split3(x):
    hi = x.astype(BF16)
    r1 = x - hi.astype(F32)
    mid = r1.astype(BF16)
    lo = (r1 - mid.astype(F32)).astype(BF16)
    return hi, mid, lo


def _cmp_body(q_ref, kcv_ref, mm_ref, gate_ref, ocmp_ref, sel_ref, *, tq, nc):
    t0 = pl.program_id(1) * tq
    q = q_ref[...]
    kc = kcv_ref[0]
    vc = kcv_ref[1]
    lo = _lane_lo()
    t = t0 + lax.broadcasted_iota(jnp.int32, (tq, 1), 0)
    cend = lax.broadcasted_iota(jnp.int32, (1, nc), 1) * CMP_STRIDE + (CMP_BLOCK - 1)
    valid = cend <= t
    imp = [jnp.zeros((tq, nc), F32), jnp.zeros((tq, nc), F32)]
    for c in range(B_HEADS // 2):
        qc = q[:, c * LANES:(c + 1) * LANES]
        outs = []
        for half in range(2):
            qm = jnp.where(lo if half == 0 else ~lo, qc, jnp.zeros_like(qc))
            s = jnp.where(valid, _dot_nt(qm, kc), NEG)
            m = jnp.max(s, axis=1, keepdims=True)
            p = jnp.where(valid, jnp.exp(s - m), 0.0)
            l = jnp.sum(p, axis=1, keepdims=True)
            pn = p * (1.0 / jnp.where(l > 0, l, 1.0))
            imp[half] = imp[half] + pn
            outs.append(_dot(pn.astype(BF16), vc))
        oc = jnp.where(lo, outs[0], outs[1])
        ocmp_ref[:, c * LANES:(c + 1) * LANES] = oc * gate_ref[:, c * LANES:(c + 1) * LANES].astype(F32)

    score = jnp.zeros((tq, LANES), F32)
    for half in range(2):
        for part in _split3(imp[half]):
            score = score + _dot(part, mm_ref[half])
    st = score.T
    row = lax.broadcasted_iota(jnp.int32, (LANES, 1), 0)
    jb = jnp.where(row < SEL_BLOCK, row, row - SEL_BLOCK)
    tl = t0 + lax.broadcasted_iota(jnp.int32, (1, tq), 1)
    cur = tl // SEL_BLOCK
    forced = (jb == 0) | (jb == cur) | (jb == cur - 1)
    st = jnp.where(forced, FORCE_SCORE, jnp.where(jb * SEL_BLOCK <= tl, st, -1.0))
    sub = 8
    sub_row = lax.broadcasted_iota(jnp.int32, (sub, 1), 0)
    picked = []
    for half in range(2):
        chunks = [st[half * SEL_BLOCK + k * sub:half * SEL_BLOCK + (k + 1) * sub] for k in range(SEL_BLOCK // sub)]
        ranks = [jnp.zeros((sub, tq), F32) for _ in chunks]
        for jp in range(SEL_BLOCK):
            other = chunks[jp // sub][jp % sub:jp % sub + 1]
            for k, ch in enumerate(chunks):
                if k < jp // sub:
                    beats = other > ch
                elif k > jp // sub:
                    beats = other >= ch
                else:
                    beats = (other > ch) | ((other == ch) & (sub_row > jp % sub))
                ranks[k] = ranks[k] + jnp.where(beats, 1.0, 0.0)
        picked += [jnp.where(r < N_SELECT, 1.0, 0.0) for r in ranks]
    sel_ref[...] = jnp.concatenate(picked, axis=0).T.astype(sel_ref.dtype)


def _cmp_select(p3, kcv, mm, g3, *, tq=256):
    bsz, seq, _ = p3.shape
    nc = kcv.shape[2]
    return pl.pallas_call(
        functools.partial(_cmp_body, tq=tq, nc=nc),
        grid=(bsz, seq // tq),
        in_specs=[
            pl.BlockSpec((None, tq, B_WIDTH), lambda b, i: (b, i, 0)),
            pl.BlockSpec((None, 2, nc, LANES), lambda b, i: (b, 0, 0, 0)),
            pl.BlockSpec((2, nc, LANES), lambda b, i: (0, 0, 0)),
            pl.BlockSpec((None, tq, B_WIDTH), lambda b, i: (b, i, 6)),
        ],
        out_specs=[
            pl.BlockSpec((None, tq, B_WIDTH), lambda b, i: (b, i, 0)),
            pl.BlockSpec((None, tq, LANES), lambda b, i: (b, i, 0)),
        ],
        out_shape=[
            jax.ShapeDtypeStruct((bsz, seq, B_WIDTH), F32),
            jax.ShapeDtypeStruct((bsz, seq, LANES), BF16),
        ],
        compiler_params=_cparams(("parallel", "parallel")),
        name="nsa_cmp_select",
    )(p3, kcv, mm, g3)


def _selwin_body(q_ref, ks_ref, vs_ref, kw_ref, vw_ref, sel_ref, ocmp_ref, gs_ref, gw_ref, o_ref,
                 m_ref, l_ref, acc_ref, qs_ref, *, tq, tk):
    t0 = pl.program_id(1) * tq
    nh = B_HEADS // 2
    qall = q_ref[...]
    lo = _lane_lo()
    trow = t0 + lax.broadcasted_iota(jnp.int32, (tq, 1), 0)
    span = WINDOW + tq
    wstart = pl.multiple_of(jnp.maximum(t0 - WINDOW, 0), tq)
    dw = trow - (wstart + lax.broadcasted_iota(jnp.int32, (1, span), 1))
    wbias = jnp.where((dw >= 0) & (dw < WINDOW), 0.0, NEG)
    nkt = (t0 + tq + tk - 1) // tk
    nblk = tk // SEL_BLOCK
    self32 = sel_ref[...].astype(F32)
    expand = jnp.where(lax.broadcasted_iota(jnp.int32, (LANES, tk), 0)
                       == lax.broadcasted_iota(jnp.int32, (LANES, tk), 1) // SEL_BLOCK, 1.0, 0.0).astype(BF16)
    hrows = [slice(c * tq, (c + 1) * tq) for c in range(nh)]
    o_win = []
    for half in range(2):
        hm = lo if half == 0 else ~lo
        qs = jnp.concatenate(
            [jnp.where(hm, qall[:, c * LANES:(c + 1) * LANES], jnp.zeros((tq, LANES), BF16)) for c in range(nh)], axis=0)

        kw = kw_ref[pl.ds(wstart, span), :]
        vw = vw_ref[pl.ds(wstart, span), :]
        s = _dot_nt(qs, kw)
        ps, ls = [], []
        for r in hrows:
            sc = s[r] + wbias
            p = jnp.exp(sc - jnp.max(sc, axis=1, keepdims=True))
            ls.append(jnp.sum(p, axis=1, keepdims=True))
            ps.append(p.astype(BF16))
        ow = _dot(jnp.concatenate(ps, axis=0), vw)
        o_win.append([ow[r] * (1.0 / l) for r, l in zip(hrows, ls)])
        qs_ref[half] = qs

    m_ref[...] = jnp.full(m_ref.shape, NEG, F32)
    l_ref[...] = jnp.zeros(l_ref.shape, F32)
    acc_ref[...] = jnp.zeros(acc_ref.shape, F32)

    def kt_body(kt, carry):
        koff = pl.multiple_of(kt * tk, tk)
        ks = ks_ref[pl.ds(koff, tk), :]
        vs = vs_ref[pl.ds(koff, tk), :]
        causal = (koff + lax.broadcasted_iota(jnp.int32, (1, tk), 1)) <= trow
        for half in range(2):
            flags = pltpu.roll(self32, (2 * LANES - half * SEL_BLOCK - kt * nblk) % LANES, axis=1).astype(BF16)
            bias = jnp.where((_dot(flags, expand) > 0.5) & causal, 0.0, NEG)
            s = _dot_nt(qs_ref[half], ks)
            ps = []
            for r in hrows:
                sc = s[r] + bias
                m_old = m_ref[half, r, :]
                m_new = jnp.maximum(m_old, jnp.max(sc, axis=1, keepdims=True))
                alpha = jnp.exp(m_old - m_new)
                p = jnp.exp(sc - jnp.concatenate([m_new] * (tk // LANES), axis=1))
                m_ref[half, r, :] = m_new
                l_ref[half, r, :] = alpha * l_ref[half, r, :] + jnp.sum(p, axis=1, keepdims=True)
                acc_ref[half, r, :] = alpha * acc_ref[half, r, :]
                ps.append(p.astype(BF16))
            acc_ref[half] += _dot(jnp.concatenate(ps, axis=0), vs)
        return carry

    lax.fori_loop(0, nkt, kt_body, 0)

    for c, r in enumerate(hrows):
        cs = slice(c * LANES, (c + 1) * LANES)
        osel = jnp.where(lo, acc_ref[0, r, :] / l_ref[0, r, :], acc_ref[1, r, :] / l_ref[1, r, :])
        owin = jnp.where(lo, o_win[0][c], o_win[1][c])
        o_ref[:, cs] = (ocmp_ref[:, cs] + gs_ref[:, cs].astype(F32) * osel
                        + gw_ref[:, cs].astype(F32) * owin).astype(o_ref.dtype)


def _sel_win(r3, p3, sel, ocmp, g3, *, tq=128, tk=512):
    bsz, seq, _ = r3.shape
    rows = (B_HEADS // 2) * tq
    full = lambda off: pl.BlockSpec((None, seq, LANES), lambda b, i: (b, 0, off))
    tile = lambda off: pl.BlockSpec((None, tq, B_WIDTH), lambda b, i: (b, i, off))
    return pl.pallas_call(
        functools.partial(_selwin_body, tq=tq, tk=tk),
        grid=(bsz, seq // tq),
        in_specs=[tile(0), full(R_KSLC), full(P_VSLC), full(R_KWIN), full(P_VWIN),
                  pl.BlockSpec((None, tq, LANES), lambda b, i: (b, i, 0)), tile(0), tile(7), tile(8)],
        out_specs=tile(0),
        out_shape=jax.ShapeDtypeStruct((bsz, seq, B_WIDTH), BF16),
        scratch_shapes=[pltpu.VMEM((2, rows, LANES), F32)] * 3 + [pltpu.VMEM((2, rows, LANES), BF16)],
        compiler_params=_cparams(("parallel", "parallel")),
        name="nsa_sel_win",
    )(r3, r3, p3, r3, p3, sel, ocmp, g3, g3)


def _stick_body(q_ref, k_ref, v_ref, o_ref, carry_ref, acc_ref, *, tq, tk):
    i = pl.program_id(2)
    t0 = i * tq
    q = q_ref[...]
    lo = _lane_lo()
    qh = [jnp.where(lo, q, jnp.zeros_like(q)), jnp.where(lo, jnp.zeros_like(q), q)]
    tpos = t0 + lax.broadcasted_iota(jnp.int32, (tq, 1), 0)
    later = jnp.where(lax.broadcasted_iota(jnp.int32, (tk, tk), 0) > lax.broadcasted_iota(jnp.int32, (tk, tk), 1),
                      1.0, 0.0).astype(BF16)
    carry_ref[...] = jnp.zeros_like(carry_ref)
    acc_ref[...] = jnp.zeros_like(acc_ref)
    ndiag = tq // tk

    def tile(koff, r0, r1, masked):
        k = k_ref[pl.ds(koff, tk), :]
        v = v_ref[pl.ds(koff, tk), :]
        rs = slice(r0, r1)
        if masked:
            before = (koff + lax.broadcasted_iota(jnp.int32, (1, tk), 1)) < tpos[rs]
        for half in range(2):
            z = _dot_nt(qh[half][rs], k)
            sp = jnp.maximum(z, 0.0) + jnp.log(1.0 + jnp.exp(-jnp.abs(z)))
            spm = jnp.where(before, sp, 0.0) if masked else sp
            inside = _dot(spm.astype(BF16), later)
            c = carry_ref[half, rs, :]
            a = jnp.exp((z - sp) + (jnp.concatenate([c] * (tk // LANES), axis=1) - inside))
            if masked:
                a = jnp.where(before, a, 0.0)
            acc_ref[half, rs, :] += _dot(a.astype(BF16), v)
            carry_ref[half, rs, :] = c - jnp.sum(spm, axis=1, keepdims=True)

    for d in reversed(range(ndiag)):
        koff = pl.multiple_of(t0 + d * tk, tk)
        tile(koff, d * tk, (d + 1) * tk, True)
        if (d + 1) * tk < tq:
            tile(koff, (d + 1) * tk, tq, False)

    def body(n, c):
        tile(pl.multiple_of(t0 - (n + 1) * tk, tk), 0, tq, False)
        return c

    lax.fori_loop(0, i * ndiag, body, 0)
    o_ref[...] = jnp.where(lo, acc_ref[0], acc_ref[1]).astype(o_ref.dtype)


def _stick(p3, *, tq=1024, tk=256):
    bsz, seq, _ = p3.shape
    npair = C_HEADS // 2
    full = lambda off: pl.BlockSpec((None, seq, LANES), lambda b, p, i: (b, 0, off + p))
    return pl.pallas_call(
        functools.partial(_stick_body, tq=tq, tk=tk),
        grid=(bsz, npair, seq // tq),
        in_specs=[pl.BlockSpec((None, tq, LANES), lambda b, p, i: (b, i, P_CQ + p)), full(P_CK), full(P_CV)],
        out_specs=pl.BlockSpec((None, tq, LANES), lambda b, p, i: (b, i, p)),
        out_shape=jax.ShapeDtypeStruct((bsz, seq, C_WIDTH), BF16),
        scratch_shapes=[pltpu.VMEM((2, tq, LANES), F32)] * 2,
        compiler_params=_cparams(("parallel", "parallel", "parallel")),
        name="stick_breaking",
    )(p3, p3, p3)


def _merge_body(h_ref, oa_ref, ob_ref, oc_ref, ga_ref, gb_ref, gc_ref, wa_ref, wb_ref, wc_ref, wo_ref, o_ref):
    y = (ga_ref[...].astype(F32) * _dot(oa_ref[...], wa_ref[...])
         + gb_ref[...].astype(F32) * _dot(ob_ref[...], wb_ref[...])
         + gc_ref[...].astype(F32) * _dot(oc_ref[...], wc_ref[...]))
    o_ref[...] = h_ref[...] + _dot(y.astype(BF16), wo_ref[...])


def _merge(h, oa, ob, oc, gates, wa, wb, wc, wo, *, tm=512):
    n, d = h.shape
    row = lambda w: pl.BlockSpec((tm, w), lambda i: (i, 0))
    whole = lambda a: pl.BlockSpec(a.shape, lambda i: (0, 0))
    gate = lambda m: pl.BlockSpec((tm, d), lambda i: (i, m))
    return pl.pallas_call(
        _merge_body,
        grid=(n // tm,),
        in_specs=[row(d), row(A_WIDTH), row(B_WIDTH), row(C_WIDTH), gate(0), gate(1), gate(2),
                  whole(wa), whole(wb), whole(wc), whole(wo)],
        out_specs=row(d),
        out_shape=jax.ShapeDtypeStruct((n, d), F32),
        compiler_params=_cparams(("parallel",)),
        name="merge_out",
    )(h, oa, ob, oc, gates, gates, gates, wa, wb, wc, wo)


_B_HEAD_ORDER = np.array([0, 4, 1, 5, 2, 6, 3, 7])


def _b_cols(base):
    return (base + (_B_HEAD_ORDER[:, None] * HEAD_DIM + np.arange(HEAD_DIM)[None, :])).reshape(-1)


def _prep_layer(w_in, w_gate, w_up, pe_k, cw1_k, cw2_k, pe_v, cw1_v, cw2_v):
    b0 = A_IN
    kv0 = b0 + B_WIDTH
    gt0 = kv0 + B_KV_IN
    c0 = b0 + B_IN
    kvcol = lambda br, kvt: kv0 + (br * 2 + kvt) * B_KV * HEAD_DIM
    span = lambda s, w: np.arange(s, s + w)
    wq_b = w_in[:, _b_cols(b0)] * Q_SCALE
    w_rope = jnp.concatenate([
        wq_b,
        w_in[:, span(0, A_GROUPS * A_WIDTH)] * Q_SCALE,
        w_in[:, span(A_GROUPS * A_WIDTH, A_GROUPS * A_WIDTH)],
        w_in[:, span(kvcol(1, 0), LANES)],
        w_in[:, span(kvcol(2, 0), LANES)],
    ], axis=1).astype(BF16)
    wc = w_in[:, c0:]
    w_plain = jnp.concatenate([
        wq_b,
        w_in[:, span(kvcol(0, 0), LANES)], w_in[:, span(kvcol(0, 1), LANES)],
        w_in[:, span(kvcol(1, 1), LANES)], w_in[:, span(kvcol(2, 1), LANES)],
        w_in[:, span(2 * A_GROUPS * A_WIDTH, A_GROUPS * A_WIDTH)],
        wc[:, :C_WIDTH] * Q_SCALE, wc[:, C_WIDTH:],
    ], axis=1).astype(BF16)
    gcols = [gt0 + 3 * np.repeat(_B_HEAD_ORDER, HEAD_DIM) + j for j in range(3)]
    w_sig = jnp.concatenate([w_gate] + [w_in[:, gc] for gc in gcols], axis=1).astype(BF16)

    def cmp_weights(pe, w1, w2):
        w1r = w1.reshape(2, CMP_STRIDE, HEAD_DIM, CMP_HIDDEN)
        z = jnp.zeros_like(w1r)
        per_head = [jnp.stack([w1r, z], axis=2), jnp.stack([z, w1r], axis=2)]
        wfull = jnp.concatenate(per_head, axis=-1).reshape(2, CMP_STRIDE * LANES, 2 * CMP_HIDDEN)
        zz = jnp.zeros_like(w2)
        w2f = jnp.concatenate([jnp.concatenate([w2, zz], axis=1), jnp.concatenate([zz, w2], axis=1)], axis=0)
        per = pe.reshape(2, CMP_STRIDE, 1, HEAD_DIM)
        pef = jnp.broadcast_to(per, (2, CMP_STRIDE, B_KV, HEAD_DIM)).reshape(2, CMP_STRIDE * LANES)
        return pef, wfull[0].astype(BF16), wfull[1].astype(BF16), w2f.astype(BF16)

    ck, cv = cmp_weights(pe_k, cw1_k, cw2_k), cmp_weights(pe_v, cw1_v, cw2_v)
    cmp_w = tuple(jnp.stack([a, b]) for a, b in zip(ck, cv))
    wb_rows = A_WIDTH + _b_cols(0)
    w_up = w_up.astype(BF16)
    return dict(w_rope=w_rope, w_plain=w_plain, w_sig=w_sig, cmp_w=cmp_w,
                wa=w_up[:A_WIDTH], wb=w_up[wb_rows], wc=w_up[A_WIDTH + B_WIDTH:])


def _rope_consts():
    inv = ROPE_THETA ** (-jnp.arange(0, ROT_DIM, 2, dtype=F32) / ROT_DIM)
    lane = np.arange(LANES) % HEAD_DIM
    half = ROT_DIM // 2
    inv_l = jnp.where(lane < ROT_DIM, inv[lane % half], 0.0).reshape(1, LANES).astype(F32)
    sa = np.where(lane < half, -1.0, 0.0).reshape(1, LANES).astype(np.float32)
    sb = np.where((lane >= half) & (lane < ROT_DIM), 1.0, 0.0).reshape(1, LANES).astype(np.float32)
    return inv_l, jnp.asarray(sa), jnp.asarray(sb)


def _block_sum_matrix(nc):
    ratio = SEL_BLOCK // CMP_STRIDE
    per = CMP_BLOCK // CMP_STRIDE
    m = np.zeros((2, nc, LANES), np.float32)
    for j in range(SEL_BLOCK):
        for a in range(ratio):
            for b in range(per):
                c = ratio * j + a + b
                if c < nc - 1:
                    m[0, c, j] += 1.0
                    m[1, c, SEL_BLOCK + j] += 1.0
    return jnp.asarray(m).astype(BF16)


def _mixers(h, lw, norm_mix, rope_in, bsz, seq):
    n = bsz * seq
    r = _proj(h, norm_mix, lw["w_rope"], "rope", rope_in, tn=R_WIDTH // 2)
    p = _proj(h, norm_mix, lw["w_plain"], "plain", tn=P_WIDTH // 2)
    g = _proj(h, norm_mix, lw["w_sig"], "sigmoid", tn=G_WIDTH // 3)
    r3, p3, g3 = r.reshape(bsz, seq, R_WIDTH), p.reshape(bsz, seq, P_WIDTH), g.reshape(bsz, seq, G_WIDTH)
    oa = _attn_a(r3, p3)
    nc = seq // CMP_STRIDE
    xc = p3[:, :, P_KCMP * LANES:(P_KCMP + 2) * LANES].reshape(bsz, nc, CMP_STRIDE, 2, LANES)
    xc = xc.transpose(0, 3, 1, 2, 4).reshape(bsz, 2, nc, CMP_STRIDE * LANES)
    kcv = _compress(xc, *lw["cmp_w"])
    ocmp, sel = _cmp_select(p3, kcv, _block_sum_matrix(nc), g3)
    ob = _sel_win(r3, p3, sel, ocmp, g3)
    oc = _stick(p3)
    return oa.reshape(n, A_WIDTH), ob.reshape(n, B_WIDTH), oc.reshape(n, C_WIDTH), g


def kernel(x, positions, norm_ffn1, ffn1_w1, ffn1_w3, ffn1_w2, norm_mix, w_in, cmp_pe_k, cmp_w1_k, cmp_w2_k,
           cmp_pe_v, cmp_w1_v, cmp_w2_v, w_gate, w_up, w_out, norm_ffn2, ffn2_w1, ffn2_w3, ffn2_w2, norm_final):
    bsz, seq, d = x.shape
    n = bsz * seq
    depth = w_in.shape[0]
    rope_in = (positions.astype(F32).reshape(n, 1),) + _rope_consts()
    h = x.reshape(n, d)
    for i in range(depth):
        lw = _prep_layer(w_in[i], w_gate[i], w_up[i], cmp_pe_k[i], cmp_w1_k[i], cmp_w2_k[i],
                         cmp_pe_v[i], cmp_w1_v[i], cmp_w2_v[i])
        h = _ffn(h, norm_ffn1[i], ffn1_w1[i].astype(BF16), ffn1_w3[i].astype(BF16), ffn1_w2[i].astype(BF16))
        oa, ob, oc, g = _mixers(h, lw, norm_mix[i], rope_in, bsz, seq)
        h = _merge(h, oa, ob, oc, g, lw["wa"], lw["wb"], lw["wc"], w_out[i].astype(BF16))
        h = _ffn(h, norm_ffn2[i], ffn2_w1[i].astype(BF16), ffn2_w3[i].astype(BF16), ffn2_w2[i].astype(BF16),
                 norm_final if i == depth - 1 else None)
    return h.reshape(bsz, seq, d)
```

```python
import functools

import numpy as np
import jax
import jax.numpy as jnp
from jax import lax
from jax.experimental import pallas as pl
from jax.experimental.pallas import tpu as pltpu

F32 = jnp.float32
BF16 = jnp.bfloat16

D_MODEL = 1024
HEAD_DIM = 64
ROT_DIM = HEAD_DIM // 4
ROPE_THETA = 500000.0
NORM_EPS = 1e-6
D_FF = 2816
LANES = 128

DIL_PAIRS = ((128, 1), (512, 4), (2048, 16))
A_GROUPS = 3
A_SLOTS = 6
A_WIDTH = A_SLOTS * HEAD_DIM
B_HEADS = 8
B_KV = 2
B_WIDTH = B_HEADS * HEAD_DIM
CMP_BLOCK = 32
CMP_STRIDE = 16
CMP_HIDDEN = 2 * HEAD_DIM
SEL_BLOCK = 64
N_SELECT = 16
WINDOW = 512
FORCE_SCORE = 1e4
C_HEADS = 6
C_WIDTH = C_HEADS * HEAD_DIM
A_IN = 3 * A_GROUPS * A_WIDTH
B_KV_IN = 3 * 2 * B_KV * HEAD_DIM
B_IN = B_WIDTH + B_KV_IN + 3 * B_HEADS
C_IN = 3 * C_WIDTH
Q_SCALE = HEAD_DIM ** -0.5

NEG = -1e30
VMEM_LIMIT = 48 * 1024 * 1024

R_WIDTH = 3072
R_AQ, R_AK, R_KSLC, R_KWIN = 4, 13, 22, 23
P_WIDTH = 3328
P_KCMP, P_VSLC, P_VWIN, P_AV, P_CQ, P_CK, P_CV = 4, 6, 7, 8, 17, 20, 23
G_WIDTH = 3328
G_BRANCH = 3 * D_MODEL // LANES


def _cparams(sem):
    return pltpu.CompilerParams(dimension_semantics=sem, vmem_limit_bytes=VMEM_LIMIT)


def _rms(x, g):
    return x * lax.rsqrt(jnp.mean(x * x, axis=-1, keepdims=True) + NORM_EPS) * g


def _dot(a, b):
    return jnp.dot(a, b, preferred_element_type=F32)


def _dot_nt(a, b):
    return lax.dot_general(a, b, (((1,), (1,)), ((), ())), preferred_element_type=F32)


def _lane_lo():
    return lax.broadcasted_iota(jnp.int32, (1, LANES), 1) < HEAD_DIM


def _branch_gate(g, branch, c, lo):
    l0 = branch * B_HEADS + 2 * c
    return jnp.where(lo, g[:, l0:l0 + 1], g[:, l0 + 1:l0 + 2])


def _ffn_body(*refs, chunks, final):
    if final:
        x_ref, g_ref, w1_ref, w3_ref, w2_ref, gf_ref, o_ref, acc_ref = refs
    else:
        x_ref, g_ref, w1_ref, w3_ref, w2_ref, o_ref, acc_ref = refs
    xn = _rms(x_ref[...], g_ref[...]).astype(BF16)
    f0 = 0
    for n, fc in enumerate(chunks):
        a = _dot(xn, w1_ref[:, f0:f0 + fc])
        b = _dot(xn, w3_ref[:, f0:f0 + fc])
        part = _dot((a * jax.nn.sigmoid(a) * b).astype(BF16), w2_ref[f0:f0 + fc, :])
        if n == 0:
            acc_ref[...] = part
        else:
            acc_ref[...] += part
        f0 += fc
    y = x_ref[...] + 0.5 * acc_ref[...]
    if final:
        y = _rms(y, gf_ref[...])
    o_ref[...] = y


def _resident(shape):
    return pl.BlockSpec(shape, lambda i: (0,) * len(shape), pipeline_mode=pl.Buffered(1))


def _ffn(x, g, w1, w3, w2, g_final=None, *, tm=512, chunks=(768, 768, 768, 512)):
    n, d = x.shape
    assert sum(chunks) == D_FF
    final = g_final is not None
    in_specs = [pl.BlockSpec((tm, d), lambda i: (i, 0)), _resident((1, d)),
                _resident(w1.shape), _resident(w3.shape), _resident(w2.shape)]
    args = [x, g.reshape(1, d), w1, w3, w2]
    if final:
        in_specs.append(_resident((1, d)))
        args.append(g_final.reshape(1, d))
    return pl.pallas_call(
        functools.partial(_ffn_body, chunks=chunks, final=final),
        grid=(n // tm,),
        in_specs=in_specs,
        out_specs=pl.BlockSpec((tm, d), lambda i: (i, 0)),
        out_shape=jax.ShapeDtypeStruct((n, d), F32),
        scratch_shapes=[pltpu.VMEM((tm, d), F32)],
        compiler_params=_cparams(("parallel",)),
        name="ffn",
    )(*args)


def _proj_body(*refs, mode, tn):
    if mode == "rope":
        x_ref, g_ref, pos_ref, inv_ref, sa_ref, sb_ref, w_ref, o_ref, xn_ref, cos_ref, sina_ref, sinb_ref = refs
    else:
        x_ref, g_ref, w_ref, o_ref, xn_ref = refs
    j = pl.program_id(1)

    @pl.when(j == 0)
    def _():
        xn_ref[...] = _rms(x_ref[...], g_ref[...]).astype(BF16)
        if mode == "rope":
            ang = pos_ref[...] * inv_ref[...]
            cos_ref[...] = jnp.cos(ang)
            s = jnp.sin(ang)
            sina_ref[...] = s * sa_ref[...]
            sinb_ref[...] = s * sb_ref[...]

    acc = _dot(xn_ref[...], w_ref[...])
    if mode == "sigmoid":
        o_ref[...] = jax.nn.sigmoid(acc).astype(o_ref.dtype)
    elif mode == "rope":
        half = ROT_DIM // 2
        cos, sina, sinb = cos_ref[...], sina_ref[...], sinb_ref[...]
        for c in range(tn // LANES):
            y = acc[:, c * LANES:(c + 1) * LANES]
            up = pltpu.roll(y, LANES - half, axis=1)
            dn = pltpu.roll(y, half, axis=1)
            o_ref[:, c * LANES:(c + 1) * LANES] = (y * cos + up * sina + dn * sinb).astype(o_ref.dtype)
    else:
        o_ref[...] = acc.astype(o_ref.dtype)


def _proj(h, g, w, mode, rope_in=None, *, tm=1024, tn=512):
    n, d = h.shape
    cols = w.shape[1]
    in_specs = [pl.BlockSpec((tm, d), lambda i, j: (i, 0)), pl.BlockSpec((1, d), lambda i, j: (0, 0))]
    args = [h, g.reshape(1, d)]
    scratch = [pltpu.VMEM((tm, d), BF16)]
    if mode == "rope":
        posf, inv, sa, sb = rope_in
        in_specs += [pl.BlockSpec((tm, 1), lambda i, j: (i, 0))] + [pl.BlockSpec((1, LANES), lambda i, j: (0, 0))] * 3
        args += [posf, inv, sa, sb]
        scratch += [pltpu.VMEM((tm, LANES), F32)] * 3
    in_specs.append(pl.BlockSpec((d, tn), lambda i, j: (0, j)))
    args.append(w)
    return pl.pallas_call(
        functools.partial(_proj_body, mode=mode, tn=tn),
        grid=(n // tm, cols // tn),
        in_specs=in_specs,
        out_specs=pl.BlockSpec((tm, tn), lambda i, j: (i, j)),
        out_shape=jax.ShapeDtypeStruct((n, cols), BF16),
        scratch_shapes=scratch,
        compiler_params=_cparams(("parallel", "arbitrary")),
        name="proj_" + mode,
    )(*args)


def _attn_a_body(q_ref, k_ref, v_ref, o_ref, qf, kf, vf, m_s, l_s, acc_s, *, seq):
    g = pl.program_id(2)
    qf[...] = q_ref[...].astype(F32)
    kf[...] = k_ref[...].astype(F32)
    vf[...] = v_ref[...].astype(F32)
    lo = _lane_lo()
    blk = LANES
    qi = lax.broadcasted_iota(jnp.int32, (blk, 1), 0) + blk
    ki = lax.broadcasted_iota(jnp.int32, (1, 2 * blk), 1)
    dist = qi - ki

    def run_group(window, dil, first):
        steps = window // dil
        per = seq // (blk * dil)
        in_band = (dist >= 0) & (dist <= steps)

        def rows(ref, start):
            if dil == 1:
                return ref[pl.ds(pl.multiple_of(start, blk), blk), :]
            return ref[pl.ds(start, blk, stride=dil), :]

        def body(idx, carry):
            r = idx // per
            j = idx - r * per
            start = r + dil * blk * j
            prev = jnp.maximum(start - dil * blk, r)
            qb = rows(qf, start)
            kcat = jnp.concatenate([rows(kf, prev), rows(kf, start)], axis=0).astype(BF16)
            vcat = jnp.concatenate([rows(vf, prev), rows(vf, start)], axis=0).astype(BF16)
            valid = in_band & ((ki >= blk) | (j > 0))
            ms, ls, os_ = [], [], []
            for half in range(2):
                qm = jnp.where(lo if half == 0 else ~lo, qb, 0.0).astype(BF16)
                s = jnp.where(valid, _dot_nt(qm, kcat), NEG)
                m = jnp.max(s, axis=1, keepdims=True)
                p = jnp.exp(s - m)
                ls.append(jnp.sum(p, axis=1, keepdims=True))
                ms.append(m)
                os_.append(_dot(p.astype(BF16), vcat))
            m_b = jnp.where(lo, ms[0], ms[1])
            l_b = jnp.where(lo, ls[0], ls[1])
            o_b = jnp.where(lo, os_[0], os_[1])
            sl = (pl.ds(pl.multiple_of(start, blk), blk) if dil == 1 else pl.ds(start, blk, stride=dil), slice(None))
            if first:
                m_s[sl] = m_b
                l_s[sl] = l_b
                acc_s[sl] = o_b
            else:
                m_old = m_s[sl]
                m_new = jnp.maximum(m_old, m_b)
                a_old = jnp.exp(m_old - m_new)
                a_b = jnp.exp(m_b - m_new)
                m_s[sl] = m_new
                l_s[sl] = l_s[sl] * a_old + l_b * a_b
                acc_s[sl] = acc_s[sl] * a_old + o_b * a_b
            return carry

        lax.fori_loop(0, seq // blk, body, 0, unroll=4)

    for gi, (window, dil) in enumerate(DIL_PAIRS):
        pl.when(g == gi)(functools.partial(run_group, window, dil, gi == 0))

    @pl.when(g == A_GROUPS - 1)
    def _():
        o_ref[...] = (acc_s[...] / l_s[...]).astype(o_ref.dtype)


def _attn_a(r3, p3):
    bsz, seq, _ = r3.shape
    npair = A_SLOTS // 2
    blk = lambda off: pl.BlockSpec((None, seq, LANES), lambda b, p, g: (b, 0, off + g * npair + p))
    return pl.pallas_call(
        functools.partial(_attn_a_body, seq=seq),
        grid=(bsz, npair, A_GROUPS),
        in_specs=[blk(R_AQ), blk(R_AK), blk(P_AV)],
        out_specs=pl.BlockSpec((None, seq, LANES), lambda b, p, g: (b, 0, p)),
        out_shape=jax.ShapeDtypeStruct((bsz, seq, A_WIDTH), BF16),
        scratch_shapes=[pltpu.VMEM((seq, LANES), F32)] * 6,
        compiler_params=_cparams(("parallel", "parallel", "arbitrary")),
        name="attn_dilated",
    )(r3, r3, p3)


def _compress_body(x_ref, pe_ref, wlo_ref, whi_ref, w2_ref, o_ref, *, nc):
    x = x_ref[...].astype(F32)
    ylo = _dot((x + pe_ref[0:1, :]).astype(BF16), wlo_ref[...])
    yhi = _dot((x + pe_ref[1:2, :]).astype(BF16), whi_ref[...])
    pre = ylo + pltpu.roll(yhi, nc - 1, axis=0)
    o_ref[...] = _dot(jax.nn.gelu(pre).astype(BF16), w2_ref[...]).astype(o_ref.dtype)


def _compress(xc, pe, wlo, whi, w2):
    bsz, _, nc, width = xc.shape
    hid = wlo.shape[-1]
    return pl.pallas_call(
        functools.partial(_compress_body, nc=nc),
        grid=(bsz, 2),
        in_specs=[
            pl.BlockSpec((None, None, nc, width), lambda b, t: (b, t, 0, 0)),
            pl.BlockSpec((None, 2, width), lambda b, t: (t, 0, 0)),
            pl.BlockSpec((None, width, hid), lambda b, t: (t, 0, 0)),
            pl.BlockSpec((None, width, hid), lambda b, t: (t, 0, 0)),
            pl.BlockSpec((None, hid, LANES), lambda b, t: (t, 0, 0)),
        ],
        out_specs=pl.BlockSpec((None, None, nc, LANES), lambda b, t: (b, t, 0, 0)),
        out_shape=jax.ShapeDtypeStruct((bsz, 2, nc, LANES), BF16),
        compiler_params=_cparams(("parallel", "parallel")),
        name="nsa_compress",
    )(xc, pe, wlo, whi, w2)


def _split3(x):
    hi = x.astype(BF16)
    r1 = x - hi.astype(F32)
    mid = r1.astype(BF16)
    lo = (r1 - mid.astype(F32)).astype(BF16)
    return hi, mid, lo


def _cmp_body(q_ref, kcv_ref, mm_ref, gate_ref, ocmp_ref, sel_ref, *, tq, nc):
    t0 = pl.program_id(1) * tq
    q = q_ref[...]
    kc = kcv_ref[0]
    vc = kcv_ref[1]
    lo = _lane_lo()
    t = t0 + lax.broadcasted_iota(jnp.int32, (tq, 1), 0)
    cend = lax.broadcasted_iota(jnp.int32, (1, nc), 1) * CMP_STRIDE + (CMP_BLOCK - 1)
    valid = cend <= t
    gates = gate_ref[...].astype(F32)
    imp = [jnp.zeros((tq, nc), F32), jnp.zeros((tq, nc), F32)]
    for c in range(B_HEADS // 2):
        qc = q[:, c * LANES:(c + 1) * LANES]
        outs = []
        for half in range(2):
            qm = jnp.where(lo if half == 0 else ~lo, qc, jnp.zeros_like(qc))
            s = jnp.where(valid, _dot_nt(qm, kc), NEG)
            m = jnp.max(s, axis=1, keepdims=True)
            p = jnp.where(valid, jnp.exp(s - m), 0.0)
            l = jnp.sum(p, axis=1, keepdims=True)
            pn = p * (1.0 / jnp.where(l > 0, l, 1.0))
            imp[half] = imp[half] + pn
            outs.append(_dot(pn.astype(BF16), vc))
        oc = jnp.where(lo, outs[0], outs[1])
        ocmp_ref[:, c * LANES:(c + 1) * LANES] = oc * _branch_gate(gates, 0, c, lo)

    score = jnp.zeros((tq, LANES), F32)
    for half in range(2):
        for part in _split3(imp[half]):
            score = score + _dot(part, mm_ref[half])
    st = score.T
    row = lax.broadcasted_iota(jnp.int32, (LANES, 1), 0)
    jb = jnp.where(row < SEL_BLOCK, row, row - SEL_BLOCK)
    tl = t0 + lax.broadcasted_iota(jnp.int32, (1, tq), 1)
    cur = tl // SEL_BLOCK
    forced = (jb == 0) | (jb == cur) | (jb == cur - 1)
    st = jnp.where(forced, FORCE_SCORE, jnp.where(jb * SEL_BLOCK <= tl, st, -1.0))
    sub = 8
    sub_row = lax.broadcasted_iota(jnp.int32, (sub, 1), 0)
    picked = []
    for half in range(2):
        chunks = [st[half * SEL_BLOCK + k * sub:half * SEL_BLOCK + (k + 1) * sub] for k in range(SEL_BLOCK // sub)]
        ranks = [jnp.zeros((sub, tq), F32) for _ in chunks]
        for jp in range(SEL_BLOCK):
            other = chunks[jp // sub][jp % sub:jp % sub + 1]
            for k, ch in enumerate(chunks):
                if k < jp // sub:
                    beats = other > ch
                elif k > jp // sub:
                    beats = other >= ch
                else:
                    beats = (other > ch) | ((other == ch) & (sub_row > jp % sub))
                ranks[k] = ranks[k] + jnp.where(beats, 1.0, 0.0)
        picked += [jnp.where(r < N_SELECT, 1.0, 0.0) for r in ranks]
    sel_ref[...] = jnp.concatenate(picked, axis=0).T.astype(sel_ref.dtype)


def _cmp_select(p3, kcv, mm, g3, *, tq=256):
    bsz, seq, _ = p3.shape
    nc = kcv.shape[2]
    return pl.pallas_call(
        functools.partial(_cmp_body, tq=tq, nc=nc),
        grid=(bsz, seq // tq),
        in_specs=[
            pl.BlockSpec((None, tq, B_WIDTH), lambda b, i: (b, i, 0)),
            pl.BlockSpec((None, 2, nc, LANES), lambda b, i: (b, 0, 0, 0)),
            pl.BlockSpec((2, nc, LANES), lambda b, i: (0, 0, 0)),
            pl.BlockSpec((None, tq, LANES), lambda b, i: (b, i, G_BRANCH)),
        ],
        out_specs=[
            pl.BlockSpec((None, tq, B_WIDTH), lambda b, i: (b, i, 0)),
            pl.BlockSpec((None, tq, LANES), lambda b, i: (b, i, 0)),
        ],
        out_shape=[
            jax.ShapeDtypeStruct((bsz, seq, B_WIDTH), F32),
            jax.ShapeDtypeStruct((bsz, seq, LANES), BF16),
        ],
        compiler_params=_cparams(("parallel", "parallel")),
        name="nsa_cmp_select",
    )(p3, kcv, mm, g3)


def _selwin_body(q_ref, ks_ref, vs_ref, kw_ref, vw_ref, sel_ref, ocmp_ref, gate_ref, o_ref,
                 m_ref, l_ref, acc_ref, qs_ref, *, tq, tk):
    t0 = pl.program_id(1) * tq
    nh = B_HEADS // 2
    qall = q_ref[...]
    lo = _lane_lo()
    trow = t0 + lax.broadcasted_iota(jnp.int32, (tq, 1), 0)
    span = WINDOW + tq
    wstart = pl.multiple_of(jnp.maximum(t0 - WINDOW, 0), tq)
    dw = trow - (wstart + lax.broadcasted_iota(jnp.int32, (1, span), 1))
    wbias = jnp.where((dw >= 0) & (dw < WINDOW), 0.0, NEG)
    nkt = (t0 + tq + tk - 1) // tk
    nblk = tk // SEL_BLOCK
    self32 = sel_ref[...].astype(F32)
    expand = jnp.where(lax.broadcasted_iota(jnp.int32, (LANES, tk), 0)
                       == lax.broadcasted_iota(jnp.int32, (LANES, tk), 1) // SEL_BLOCK, 1.0, 0.0).astype(BF16)
    hrows = [slice(c * tq, (c + 1) * tq) for c in range(nh)]
    o_win = []
    for half in range(2):
        hm = lo if half == 0 else ~lo
        qs = jnp.concatenate(
            [jnp.where(hm, qall[:, c * LANES:(c + 1) * LANES], jnp.zeros((tq, LANES), BF16)) for c in range(nh)], axis=0)

        kw = kw_ref[pl.ds(wstart, span), :]
        vw = vw_ref[pl.ds(wstart, span), :]
        s = _dot_nt(qs, kw)
        ps, ls = [], []
        for r in hrows:
            sc = s[r] + wbias
            p = jnp.exp(sc - jnp.max(sc, axis=1, keepdims=True))
            ls.append(jnp.sum(p, axis=1, keepdims=True))
            ps.append(p.astype(BF16))
        ow = _dot(jnp.concatenate(ps, axis=0), vw)
        o_win.append([ow[r] * (1.0 / l) for r, l in zip(hrows, ls)])
        qs_ref[half] = qs

    m_ref[...] = jnp.full(m_ref.shape, NEG, F32)
    l_ref[...] = jnp.zeros(l_ref.shape, F32)
    acc_ref[...] = jnp.zeros(acc_ref.shape, F32)

    def kt_body(kt, carry):
        koff = pl.multiple_of(kt * tk, tk)
        ks = ks_ref[pl.ds(koff, tk), :]
        vs = vs_ref[pl.ds(koff, tk), :]
        causal = (koff + lax.broadcasted_iota(jnp.int32, (1, tk), 1)) <= trow
        for half in range(2):
            flags = pltpu.roll(self32, (2 * LANES - half * SEL_BLOCK - kt * nblk) % LANES, axis=1).astype(BF16)
            bias = jnp.where((_dot(flags, expand) > 0.5) & causal, 0.0, NEG)
            s = _dot_nt(qs_ref[half], ks)
            ps = []
            for r in hrows:
                sc = s[r] + bias
                m_old = m_ref[half, r, :]
                m_new = jnp.maximum(m_old, jnp.max(sc, axis=1, keepdims=True))
                alpha = jnp.exp(m_old - m_new)
                p = jnp.exp(sc - jnp.concatenate([m_new] * (tk // LANES), axis=1))
                m_ref[half, r, :] = m_new
                l_ref[half, r, :] = alpha * l_ref[half, r, :] + jnp.sum(p, axis=1, keepdims=True)
                acc_ref[half, r, :] = alpha * acc_ref[half, r, :]
                ps.append(p.astype(BF16))
            acc_ref[half] += _dot(jnp.concatenate(ps, axis=0), vs)
        return carry

    lax.fori_loop(0, nkt, kt_body, 0)

    gates = gate_ref[...].astype(F32)
    for c, r in enumerate(hrows):
        cs = slice(c * LANES, (c + 1) * LANES)
        osel = jnp.where(lo, acc_ref[0, r, :] / l_ref[0, r, :], acc_ref[1, r, :] / l_ref[1, r, :])
        owin = jnp.where(lo, o_win[0][c], o_win[1][c])
        o_ref[:, cs] = (ocmp_ref[:, cs] + _branch_gate(gates, 1, c, lo) * osel
                        + _branch_gate(gates, 2, c, lo) * owin).astype(o_ref.dtype)


def _sel_win(r3, p3, sel, ocmp, g3, *, tq=128, tk=512):
    bsz, seq, _ = r3.shape
    rows = (B_HEADS // 2) * tq
    full = lambda off: pl.BlockSpec((None, seq, LANES), lambda b, i: (b, 0, off))
    tile = lambda off: pl.BlockSpec((None, tq, B_WIDTH), lambda b, i: (b, i, off))
    return pl.pallas_call(
        functools.partial(_selwin_body, tq=tq, tk=tk),
        grid=(bsz, seq // tq),
        in_specs=[tile(0), full(R_KSLC), full(P_VSLC), full(R_KWIN), full(P_VWIN),
                  pl.BlockSpec((None, tq, LANES), lambda b, i: (b, i, 0)), tile(0),
                  pl.BlockSpec((None, tq, LANES), lambda b, i: (b, i, G_BRANCH))],
        out_specs=tile(0),
        out_shape=jax.ShapeDtypeStruct((bsz, seq, B_WIDTH), BF16),
        scratch_shapes=[pltpu.VMEM((2, rows, LANES), F32)] * 3 + [pltpu.VMEM((2, rows, LANES), BF16)],
        compiler_params=_cparams(("parallel", "parallel")),
        name="nsa_sel_win",
    )(r3, r3, p3, r3, p3, sel, ocmp, g3)


def _stick_body(q_ref, k_ref, v_ref, o_ref, carry_ref, acc_ref, *, tq, tk):
    i = pl.program_id(2)
    t0 = i * tq
    q = q_ref[...]
    lo = _lane_lo()
    qh = [jnp.where(lo, q, jnp.zeros_like(q)), jnp.where(lo, jnp.zeros_like(q), q)]
    tpos = t0 + lax.broadcasted_iota(jnp.int32, (tq, 1), 0)
    later = jnp.where(lax.broadcasted_iota(jnp.int32, (tk, tk), 0) > lax.broadcasted_iota(jnp.int32, (tk, tk), 1),
                      1.0, 0.0).astype(BF16)
    carry_ref[...] = jnp.zeros_like(carry_ref)
    acc_ref[...] = jnp.zeros_like(acc_ref)
    ndiag = tq // tk

    def tile(koff, r0, r1, masked):
        k = k_ref[pl.ds(koff, tk), :]
        v = v_ref[pl.ds(koff, tk), :]
        rs = slice(r0, r1)
        if masked:
            before = (koff + lax.broadcasted_iota(jnp.int32, (1, tk), 1)) < tpos[rs]
        for half in range(2):
            z = _dot_nt(qh[half][rs], k)
            sp = jnp.maximum(z, 0.0) + jnp.log(1.0 + jnp.exp(-jnp.abs(z)))
            spm = jnp.where(before, sp, 0.0) if masked else sp
            inside = _dot(spm.astype(BF16), later)
            c = carry_ref[half, rs, :]
            a = jnp.exp((z - sp) + (jnp.concatenate([c] * (tk // LANES), axis=1) - inside))
            if masked:
                a = jnp.where(before, a, 0.0)
            acc_ref[half, rs, :] += _dot(a.astype(BF16), v)
            carry_ref[half, rs, :] = c - jnp.sum(spm, axis=1, keepdims=True)

    for d in reversed(range(ndiag)):
        koff = pl.multiple_of(t0 + d * tk, tk)
        tile(koff, d * tk, (d + 1) * tk, True)
        if (d + 1) * tk < tq:
            tile(koff, (d + 1) * tk, tq, False)

    def body(n, c):
        tile(pl.multiple_of(t0 - (n + 1) * tk, tk), 0, tq, False)
        return c

    lax.fori_loop(0, i * ndiag, body, 0)
    o_ref[...] = jnp.where(lo, acc_ref[0], acc_ref[1]).astype(o_ref.dtype)


def _stick(p3, *, tq=1024, tk=256):
    bsz, seq, _ = p3.shape
    npair = C_HEADS // 2
    full = lambda off: pl.BlockSpec((None, seq, LANES), lambda b, p, i: (b, 0, off + p))
    return pl.pallas_call(
        functools.partial(_stick_body, tq=tq, tk=tk),
        grid=(bsz, npair, seq // tq),
        in_specs=[pl.BlockSpec((None, tq, LANES), lambda b, p, i: (b, i, P_CQ + p)), full(P_CK), full(P_CV)],
        out_specs=pl.BlockSpec((None, tq, LANES), lambda b, p, i: (b, i, p)),
        out_shape=jax.ShapeDtypeStruct((bsz, seq, C_WIDTH), BF16),
        scratch_shapes=[pltpu.VMEM((2, tq, LANES), F32)] * 2,
        compiler_params=_cparams(("parallel", "parallel", "parallel")),
        name="stick_breaking",
    )(p3, p3, p3)


def _merge_body(h_ref, oa_ref, ob_ref, oc_ref, ga_ref, gb_ref, gc_ref, wa_ref, wb_ref, wc_ref, wo_ref, o_ref):
    y = (ga_ref[...].astype(F32) * _dot(oa_ref[...], wa_ref[...])
         + gb_ref[...].astype(F32) * _dot(ob_ref[...], wb_ref[...])
         + gc_ref[...].astype(F32) * _dot(oc_ref[...], wc_ref[...]))
    o_ref[...] = h_ref[...] + _dot(y.astype(BF16), wo_ref[...])


def _merge(h, oa, ob, oc, gates, wa, wb, wc, wo, *, tm=512):
    n, d = h.shape
    row = lambda w: pl.BlockSpec((tm, w), lambda i: (i, 0))
    whole = lambda a: pl.BlockSpec(a.shape, lambda i: (0, 0))
    gate = lambda m: pl.BlockSpec((tm, d), lambda i: (i, m))
    return pl.pallas_call(
        _merge_body,
        grid=(n // tm,),
        in_specs=[row(d), row(A_WIDTH), row(B_WIDTH), row(C_WIDTH), gate(0), gate(1), gate(2),
                  whole(wa), whole(wb), whole(wc), whole(wo)],
        out_specs=row(d),
        out_shape=jax.ShapeDtypeStruct((n, d), F32),
        compiler_params=_cparams(("parallel",)),
        name="merge_out",
    )(h, oa, ob, oc, gates, gates, gates, wa, wb, wc, wo)


def _prep_layer(w_in, w_gate, w_up, pe_k, cw1_k, cw2_k, pe_v, cw1_v, cw2_v):
    b0 = A_IN
    kv0 = b0 + B_WIDTH
    gt0 = kv0 + B_KV_IN
    c0 = b0 + B_IN
    d = w_in.shape[0]
    nh = B_HEADS // 2
    kvcol = lambda br, kvt: kv0 + (br * 2 + kvt) * B_KV * HEAD_DIM
    span = lambda s, w: w_in[:, s:s + w]
    wq_b = (span(b0, B_WIDTH).reshape(d, 2, nh, HEAD_DIM).transpose(0, 2, 1, 3).reshape(d, B_WIDTH) * Q_SCALE).astype(BF16)
    aw = A_GROUPS * A_WIDTH
    w_rope = jnp.concatenate([
        wq_b, (span(0, aw) * Q_SCALE).astype(BF16), span(aw, aw).astype(BF16),
        span(kvcol(1, 0), LANES).astype(BF16), span(kvcol(2, 0), LANES).astype(BF16)], axis=1)
    w_plain = jnp.concatenate([
        wq_b, span(kvcol(0, 0), 2 * LANES).astype(BF16),
        span(kvcol(1, 1), LANES).astype(BF16), span(kvcol(2, 1), LANES).astype(BF16),
        span(2 * aw, aw).astype(BF16), (span(c0, C_WIDTH) * Q_SCALE).astype(BF16),
        span(c0 + C_WIDTH, 2 * C_WIDTH).astype(BF16)], axis=1)
    wg = span(gt0, 3 * B_HEADS).reshape(d, 2, nh, 3).transpose(0, 3, 2, 1).reshape(d, 3 * B_HEADS)
    w_sig = jnp.concatenate([w_gate.astype(BF16), wg.astype(BF16),
                             jnp.zeros((d, G_WIDTH - 3 * D_MODEL - 3 * B_HEADS), BF16)], axis=1)

    def cmp_weights(pe, w1, w2):
        w1r = w1.reshape(2, CMP_STRIDE, HEAD_DIM, CMP_HIDDEN)
        z = jnp.zeros_like(w1r)
        per_head = [jnp.stack([w1r, z], axis=2), jnp.stack([z, w1r], axis=2)]
        wfull = jnp.concatenate(per_head, axis=-1).reshape(2, CMP_STRIDE * LANES, 2 * CMP_HIDDEN)
        zz = jnp.zeros_like(w2)
        w2f = jnp.concatenate([jnp.concatenate([w2, zz], axis=1), jnp.concatenate([zz, w2], axis=1)], axis=0)
        per = pe.reshape(2, CMP_STRIDE, 1, HEAD_DIM)
        pef = jnp.broadcast_to(per, (2, CMP_STRIDE, B_KV, HEAD_DIM)).reshape(2, CMP_STRIDE * LANES)
        return pef, wfull[0].astype(BF16), wfull[1].astype(BF16), w2f.astype(BF16)

    ck, cv = cmp_weights(pe_k, cw1_k, cw2_k), cmp_weights(pe_v, cw1_v, cw2_v)
    cmp_w = tuple(jnp.stack([a, b]) for a, b in zip(ck, cv))
    w_up = w_up.astype(BF16)
    wb = w_up[A_WIDTH:A_WIDTH + B_WIDTH].reshape(2, nh, HEAD_DIM, -1).transpose(1, 0, 2, 3).reshape(B_WIDTH, -1)
    return dict(w_rope=w_rope, w_plain=w_plain, w_sig=w_sig, cmp_w=cmp_w,
                wa=w_up[:A_WIDTH], wb=wb, wc=w_up[A_WIDTH + B_WIDTH:])


def _rope_consts():
    inv = ROPE_THETA ** (-jnp.arange(0, ROT_DIM, 2, dtype=F32) / ROT_DIM)
    lane = np.arange(LANES) % HEAD_DIM
    half = ROT_DIM // 2
    inv_l = jnp.where(lane < ROT_DIM, inv[lane % half], 0.0).reshape(1, LANES).astype(F32)
    sa = np.where(lane < half, -1.0, 0.0).reshape(1, LANES).astype(np.float32)
    sb = np.where((lane >= half) & (lane < ROT_DIM), 1.0, 0.0).reshape(1, LANES).astype(np.float32)
    return inv_l, jnp.asarray(sa), jnp.asarray(sb)


def _block_sum_matrix(nc):
    ratio = SEL_BLOCK // CMP_STRIDE
    per = CMP_BLOCK // CMP_STRIDE
    m = np.zeros((2, nc, LANES), np.float32)
    for j in range(SEL_BLOCK):
        for a in range(ratio):
            for b in range(per):
                c = ratio * j + a + b
                if c < nc - 1:
                    m[0, c, j] += 1.0
                    m[1, c, SEL_BLOCK + j] += 1.0
    return jnp.asarray(m).astype(BF16)


def _mixers(h, lw, norm_mix, rope_in, bsz, seq):
    n = bsz * seq
    r = _proj(h, norm_mix, lw["w_rope"], "rope", rope_in, tn=R_WIDTH // 2)
    p = _proj(h, norm_mix, lw["w_plain"], "plain", tn=P_WIDTH // 2)
    g = _proj(h, norm_mix, lw["w_sig"], "sigmoid", tn=G_WIDTH // 2)
    r3, p3, g3 = r.reshape(bsz, seq, R_WIDTH), p.reshape(bsz, seq, P_WIDTH), g.reshape(bsz, seq, G_WIDTH)
    oa = _attn_a(r3, p3)
    nc = seq // CMP_STRIDE
    xc = p3[:, :, P_KCMP * LANES:(P_KCMP + 2) * LANES].reshape(bsz, nc, CMP_STRIDE, 2, LANES)
    xc = xc.transpose(0, 3, 1, 2, 4).reshape(bsz, 2, nc, CMP_STRIDE * LANES)
    kcv = _compress(xc, *lw["cmp_w"])
    ocmp, sel = _cmp_select(p3, kcv, _block_sum_matrix(nc), g3)
    ob = _sel_win(r3, p3, sel, ocmp, g3)
    oc = _stick(p3)
    return oa.reshape(n, A_WIDTH), ob.reshape(n, B_WIDTH), oc.reshape(n, C_WIDTH), g


def kernel(x, positions, norm_ffn1, ffn1_w1, ffn1_w3, ffn1_w2, norm_mix, w_in, cmp_pe_k, cmp_w1_k, cmp_w2_k,
           cmp_pe_v, cmp_w1_v, cmp_w2_v, w_gate, w_up, w_out, norm_ffn2, ffn2_w1, ffn2_w3, ffn2_w2, norm_final):
    bsz, seq, d = x.shape
    n = bsz * seq
    depth = w_in.shape[0]
    rope_in = (positions.astype(F32).reshape(n, 1),) + _rope_consts()
    h = x.reshape(n, d)
    for i in range(depth):
        lw = _prep_layer(w_in[i], w_gate[i], w_up[i], cmp_pe_k[i], cmp_w1_k[i], cmp_w2_k[i],
                         cmp_pe_v[i], cmp_w1_v[i], cmp_w2_v[i])
        h = _ffn(h, norm_ffn1[i], ffn1_w1[i].astype(BF16), ffn1_w3[i].astype(BF16), ffn1_w2[i].astype(BF16))
        oa, ob, oc, g = _mixers(h, lw, norm_mix[i], rope_in, bsz, seq)
        h = _merge(h, oa, ob, oc, g, lw["wa"], lw["wb"], lw["wc"], w_out[i].astype(BF16))
        h = _ffn(h, norm_ffn2[i], ffn2_w1[i].astype(BF16), ffn2_w3[i].astype(BF16), ffn2_w2[i].astype(BF16),
                 norm_final if i == depth - 1 else None)
    return h.reshape(bsz, seq, d)
```

```python
import functools

import numpy as np
import jax
import jax.numpy as jnp
from jax import lax
from jax.experimental import pallas as pl
from jax.experimental.pallas import tpu as pltpu

F32 = jnp.float32
BF16 = jnp.bfloat16

D_MODEL = 1024
HEAD_DIM = 64
ROT_DIM = HEAD_DIM // 4
ROPE_THETA = 500000.0
NORM_EPS = 1e-6
D_FF = 2816
LANES = 128

DIL_PAIRS = ((128, 1), (512, 4), (2048, 16))
A_GROUPS = 3
A_SLOTS = 6
A_WIDTH = A_SLOTS * HEAD_DIM
B_HEADS = 8
B_KV = 2
B_WIDTH = B_HEADS * HEAD_DIM
CMP_BLOCK = 32
CMP_STRIDE = 16
CMP_HIDDEN = 2 * HEAD_DIM
SEL_BLOCK = 64
N_SELECT = 16
WINDOW = 512
FORCE_SCORE = 1e4
C_HEADS = 6
C_WIDTH = C_HEADS * HEAD_DIM
A_IN = 3 * A_GROUPS * A_WIDTH
B_KV_IN = 3 * 2 * B_KV * HEAD_DIM
B_IN = B_WIDTH + B_KV_IN + 3 * B_HEADS
C_IN = 3 * C_WIDTH
Q_SCALE = HEAD_DIM ** -0.5

NEG = -1e30
VMEM_LIMIT = 48 * 1024 * 1024

R_WIDTH = 3072
R_AQ, R_AK, R_KSLC, R_KWIN = 4, 13, 22, 23
P_WIDTH = 3328
P_KCMP, P_VSLC, P_VWIN, P_AV, P_CQ, P_CK, P_CV = 4, 6, 7, 8, 17, 20, 23
G_WIDTH = 3328
G_BRANCH = 3 * D_MODEL // LANES


def _cparams(sem):
    return pltpu.CompilerParams(dimension_semantics=sem, vmem_limit_bytes=VMEM_LIMIT)


def _rms(x, g):
    return x * lax.rsqrt(jnp.mean(x * x, axis=-1, keepdims=True) + NORM_EPS) * g


def _dot(a, b):
    return jnp.dot(a, b, preferred_element_type=F32)


def _dot_nt(a, b):
    return lax.dot_general(a, b, (((1,), (1,)), ((), ())), preferred_element_type=F32)


def _lane_lo():
    return lax.broadcasted_iota(jnp.int32, (1, LANES), 1) < HEAD_DIM


def _branch_gate(g, branch, c, lo):
    l0 = branch * B_HEADS + 2 * c
    return jnp.where(lo, g[:, l0:l0 + 1], g[:, l0 + 1:l0 + 2])


def _ffn_body(*refs, chunks, final):
    if final:
        x_ref, g_ref, w1_ref, w3_ref, w2_ref, gf_ref, o_ref, acc_ref = refs
    else:
        x_ref, g_ref, w1_ref, w3_ref, w2_ref, o_ref, acc_ref = refs
    xn = _rms(x_ref[...], g_ref[...]).astype(BF16)
    f0 = 0
    for n, fc in enumerate(chunks):
        a = _dot(xn, w1_ref[:, f0:f0 + fc])
        b = _dot(xn, w3_ref[:, f0:f0 + fc])
        part = _dot((a * jax.nn.sigmoid(a) * b).astype(BF16), w2_ref[f0:f0 + fc, :])
        if n == 0:
            acc_ref[...] = part
        else:
            acc_ref[...] += part
        f0 += fc
    y = x_ref[...] + 0.5 * acc_ref[...]
    if final:
        y = _rms(y, gf_ref[...])
    o_ref[...] = y


def _resident(shape):
    return pl.BlockSpec(shape, lambda i: (0,) * len(shape), pipeline_mode=pl.Buffered(1))


def _ffn(x, g, w1, w3, w2, g_final=None, *, tm=512, chunks=(768, 768, 768, 512)):
    n, d = x.shape
    assert sum(chunks) == D_FF
    final = g_final is not None
    in_specs = [pl.BlockSpec((tm, d), lambda i: (i, 0)), _resident((1, d)),
                _resident(w1.shape), _resident(w3.shape), _resident(w2.shape)]
    args = [x, g.reshape(1, d), w1, w3, w2]
    if final:
        in_specs.append(_resident((1, d)))
        args.append(g_final.reshape(1, d))
    return pl.pallas_call(
        functools.partial(_ffn_body, chunks=chunks, final=final),
        grid=(n // tm,),
        in_specs=in_specs,
        out_specs=pl.BlockSpec((tm, d), lambda i: (i, 0)),
        out_shape=jax.ShapeDtypeStruct((n, d), F32),
        scratch_shapes=[pltpu.VMEM((tm, d), F32)],
        compiler_params=_cparams(("parallel",)),
        name="ffn",
    )(*args)


def _proj_body(*refs, mode, tn):
    if mode == "rope":
        x_ref, g_ref, pos_ref, inv_ref, sa_ref, sb_ref, w_ref, o_ref, xn_ref, cos_ref, sina_ref, sinb_ref = refs
    else:
        x_ref, g_ref, w_ref, o_ref, xn_ref = refs
    j = pl.program_id(1)

    @pl.when(j == 0)
    def _():
        xn_ref[...] = _rms(x_ref[...], g_ref[...]).astype(BF16)
        if mode == "rope":
            ang = pos_ref[...] * inv_ref[...]
            cos_ref[...] = jnp.cos(ang)
            s = jnp.sin(ang)
            sina_ref[...] = s * sa_ref[...]
            sinb_ref[...] = s * sb_ref[...]

    acc = _dot(xn_ref[...], w_ref[...])
    if mode == "sigmoid":
        o_ref[...] = jax.nn.sigmoid(acc).astype(o_ref.dtype)
    elif mode == "rope":
        half = ROT_DIM // 2
        cos, sina, sinb = cos_ref[...], sina_ref[...], sinb_ref[...]
        for c in range(tn // LANES):
            y = acc[:, c * LANES:(c + 1) * LANES]
            up = pltpu.roll(y, LANES - half, axis=1)
            dn = pltpu.roll(y, half, axis=1)
            o_ref[:, c * LANES:(c + 1) * LANES] = (y * cos + up * sina + dn * sinb).astype(o_ref.dtype)
    else:
        o_ref[...] = acc.astype(o_ref.dtype)


def _proj(h, g, w, mode, rope_in=None, *, tm=1024, tn=512):
    n, d = h.shape
    cols = w.shape[1]
    in_specs = [pl.BlockSpec((tm, d), lambda i, j: (i, 0)), pl.BlockSpec((1, d), lambda i, j: (0, 0))]
    args = [h, g.reshape(1, d)]
    scratch = [pltpu.VMEM((tm, d), BF16)]
    if mode == "rope":
        posf, inv, sa, sb = rope_in
        in_specs += [pl.BlockSpec((tm, 1), lambda i, j: (i, 0))] + [pl.BlockSpec((1, LANES), lambda i, j: (0, 0))] * 3
        args += [posf, inv, sa, sb]
        scratch += [pltpu.VMEM((tm, LANES), F32)] * 3
    in_specs.append(pl.BlockSpec((d, tn), lambda i, j: (0, j)))
    args.append(w)
    return pl.pallas_call(
        functools.partial(_proj_body, mode=mode, tn=tn),
        grid=(n // tm, cols // tn),
        in_specs=in_specs,
        out_specs=pl.BlockSpec((tm, tn), lambda i, j: (i, j)),
        out_shape=jax.ShapeDtypeStruct((n, cols), BF16),
        scratch_shapes=scratch,
        compiler_params=_cparams(("parallel", "arbitrary")),
        name="proj_" + mode,
    )(*args)


def _attn_a_body(q_ref, k_ref, v_ref, o_ref, qf, kf, vf, m_s, l_s, acc_s, *, seq):
    g = pl.program_id(2)
    qf[...] = q_ref[...].astype(F32)
    kf[...] = k_ref[...].astype(F32)
    vf[...] = v_ref[...].astype(F32)
    lo = _lane_lo()
    blk = LANES
    qi = lax.broadcasted_iota(jnp.int32, (blk, 1), 0) + blk
    ki = lax.broadcasted_iota(jnp.int32, (1, 2 * blk), 1)
    dist = qi - ki

    def run_group(window, dil, first):
        steps = window // dil
        per = seq // (blk * dil)
        in_band = (dist >= 0) & (dist <= steps)

        def rows(ref, start):
            if dil == 1:
                return ref[pl.ds(pl.multiple_of(start, blk), blk), :]
            return ref[pl.ds(start, blk, stride=dil), :]

        def body(idx, carry):
            r = idx // per
            j = idx - r * per
            start = r + dil * blk * j
            prev = jnp.maximum(start - dil * blk, r)
            qb = rows(qf, start)
            kcat = jnp.concatenate([rows(kf, prev), rows(kf, start)], axis=0).astype(BF16)
            vcat = jnp.concatenate([rows(vf, prev), rows(vf, start)], axis=0).astype(BF16)
            valid = in_band & ((ki >= blk) | (j > 0))
            ms, ls, os_ = [], [], []
            for half in range(2):
                qm = jnp.where(lo if half == 0 else ~lo, qb, 0.0).astype(BF16)
                s = jnp.where(valid, _dot_nt(qm, kcat), NEG)
                m = jnp.max(s, axis=1, keepdims=True)
                p = jnp.exp(s - m)
                ls.append(jnp.sum(p, axis=1, keepdims=True))
                ms.append(m)
                os_.append(_dot(p.astype(BF16), vcat))
            m_b = jnp.where(lo, ms[0], ms[1])
            l_b = jnp.where(lo, ls[0], ls[1])
            o_b = jnp.where(lo, os_[0], os_[1])
            sl = (pl.ds(pl.multiple_of(start, blk), blk) if dil == 1 else pl.ds(start, blk, stride=dil), slice(None))
            if first:
                m_s[sl] = m_b
                l_s[sl] = l_b
                acc_s[sl] = o_b
            else:
                m_old = m_s[sl]
                m_new = jnp.maximum(m_old, m_b)
                a_old = jnp.exp(m_old - m_new)
                a_b = jnp.exp(m_b - m_new)
                m_s[sl] = m_new
                l_s[sl] = l_s[sl] * a_old + l_b * a_b
                acc_s[sl] = acc_s[sl] * a_old + o_b * a_b
            return carry

        lax.fori_loop(0, seq // blk, body, 0, unroll=4)

    for gi, (window, dil) in enumerate(DIL_PAIRS):
        pl.when(g == gi)(functools.partial(run_group, window, dil, gi == 0))

    @pl.when(g == A_GROUPS - 1)
    def _():
        o_ref[...] = (acc_s[...] / l_s[...]).astype(o_ref.dtype)


def _attn_a(r3, p3):
    bsz, seq, _ = r3.shape
    npair = A_SLOTS // 2
    blk = lambda off: pl.BlockSpec((None, seq, LANES), lambda b, p, g: (b, 0, off + g * npair + p))
    return pl.pallas_call(
        functools.partial(_attn_a_body, seq=seq),
        grid=(bsz, npair, A_GROUPS),
        in_specs=[blk(R_AQ), blk(R_AK), blk(P_AV)],
        out_specs=pl.BlockSpec((None, seq, LANES), lambda b, p, g: (b, 0, p)),
        out_shape=jax.ShapeDtypeStruct((bsz, seq, A_WIDTH), BF16),
        scratch_shapes=[pltpu.VMEM((seq, LANES), F32)] * 6,
        compiler_params=_cparams(("parallel", "parallel", "arbitrary")),
        name="attn_dilated",
    )(r3, r3, p3)


def _compress_body(x_ref, pe_ref, wlo_ref, whi_ref, w2_ref, o_ref, *, nc):
    x = x_ref[...].astype(F32)
    ylo = _dot((x + pe_ref[0:1, :]).astype(BF16), wlo_ref[...])
    yhi = _dot((x + pe_ref[1:2, :]).astype(BF16), whi_ref[...])
    pre = ylo + pltpu.roll(yhi, nc - 1, axis=0)
    o_ref[...] = _dot(jax.nn.gelu(pre).astype(BF16), w2_ref[...]).astype(o_ref.dtype)


def _compress(xc, pe, wlo, whi, w2):
    bsz, _, nc, width = xc.shape
    hid = wlo.shape[-1]
    return pl.pallas_call(
        functools.partial(_compress_body, nc=nc),
        grid=(bsz, 2),
        in_specs=[
            pl.BlockSpec((None, None, nc, width), lambda b, t: (b, t, 0, 0)),
            pl.BlockSpec((None, 2, width), lambda b, t: (t, 0, 0)),
            pl.BlockSpec((None, width, hid), lambda b, t: (t, 0, 0)),
            pl.BlockSpec((None, width, hid), lambda b, t: (t, 0, 0)),
            pl.BlockSpec((None, hid, LANES), lambda b, t: (t, 0, 0)),
        ],
        out_specs=pl.BlockSpec((None, None, nc, LANES), lambda b, t: (b, t, 0, 0)),
        out_shape=jax.ShapeDtypeStruct((bsz, 2, nc, LANES), BF16),
        compiler_params=_cparams(("parallel", "parallel")),
        name="nsa_compress",
    )(xc, pe, wlo, whi, w2)


def _split3(x):
    hi = x.astype(BF16)
    r1 = x - hi.astype(F32)
    mid = r1.astype(BF16)
    lo = (r1 - mid.astype(F32)).astype(BF16)
    return hi, mid, lo


def _cmp_body(q_ref, kcv_ref, mm_ref, gate_ref, ocmp_ref, sel_ref, *, tq, nc):
    t0 = pl.program_id(1) * tq
    q = q_ref[...]
    kc = kcv_ref[0]
    vc = kcv_ref[1]
    lo = _lane_lo()
    t = t0 + lax.broadcasted_iota(jnp.int32, (tq, 1), 0)
    cend = lax.broadcasted_iota(jnp.int32, (1, nc), 1) * CMP_STRIDE + (CMP_BLOCK - 1)
    valid = cend <= t
    gates = gate_ref[...].astype(F32)
    imp = [jnp.zeros((tq, nc), F32), jnp.zeros((tq, nc), F32)]
    for c in range(B_HEADS // 2):
        qc = q[:, c * LANES:(c + 1) * LANES]
        outs = []
        for half in range(2):
            qm = jnp.where(lo if half == 0 else ~lo, qc, jnp.zeros_like(qc))
            s = jnp.where(valid, _dot_nt(qm, kc), NEG)
            m = jnp.max(s, axis=1, keepdims=True)
            p = jnp.where(valid, jnp.exp(s - m), 0.0)
            l = jnp.sum(p, axis=1, keepdims=True)
            pn = p * (1.0 / jnp.where(l > 0, l, 1.0))
            imp[half] = imp[half] + pn
            outs.append(_dot(pn.astype(BF16), vc))
        oc = jnp.where(lo, outs[0], outs[1])
        ocmp_ref[:, c * LANES:(c + 1) * LANES] = oc * _branch_gate(gates, 0, c, lo)

    score = jnp.zeros((tq, LANES), F32)
    for half in range(2):
        for part in _split3(imp[half]):
            score = score + _dot(part, mm_ref[half])
    st = score.T
    row = lax.broadcasted_iota(jnp.int32, (LANES, 1), 0)
    jb = jnp.where(row < SEL_BLOCK, row, row - SEL_BLOCK)
    tl = t0 + lax.broadcasted_iota(jnp.int32, (1, tq), 1)
    cur = tl // SEL_BLOCK
    forced = (jb == 0) | (jb == cur) | (jb == cur - 1)
    st = jnp.where(forced, FORCE_SCORE, jnp.where(jb * SEL_BLOCK <= tl, st, -1.0))
    sub = 8
    sub_row = lax.broadcasted_iota(jnp.int32, (sub, 1), 0)
    picked = []
    for half in range(2):
        chunks = [st[half * SEL_BLOCK + k * sub:half * SEL_BLOCK + (k + 1) * sub] for k in range(SEL_BLOCK // sub)]
        ranks = [jnp.zeros((sub, tq), F32) for _ in chunks]
        for jp in range(SEL_BLOCK):
            other = chunks[jp // sub][jp % sub:jp % sub + 1]
            for k, ch in enumerate(chunks):
                if k < jp // sub:
                    beats = other > ch
                elif k > jp // sub:
                    beats = other >= ch
                else:
                    beats = (other > ch) | ((other == ch) & (sub_row > jp % sub))
                ranks[k] = ranks[k] + jnp.where(beats, 1.0, 0.0)
        picked += [jnp.where(r < N_SELECT, 1.0, 0.0) for r in ranks]
    sel_ref[...] = jnp.concatenate(picked, axis=0)


def _cmp_select(p3, kcv, mm, g3, *, tq=256):
    bsz, seq, _ = p3.shape
    nc = kcv.shape[2]
    return pl.pallas_call(
        functools.partial(_cmp_body, tq=tq, nc=nc),
        grid=(bsz, seq // tq),
        in_specs=[
            pl.BlockSpec((None, tq, B_WIDTH), lambda b, i: (b, i, 0)),
            pl.BlockSpec((None, 2, nc, LANES), lambda b, i: (b, 0, 0, 0)),
            pl.BlockSpec((2, nc, LANES), lambda b, i: (0, 0, 0)),
            pl.BlockSpec((None, tq, LANES), lambda b, i: (b, i, G_BRANCH)),
        ],
        out_specs=[
            pl.BlockSpec((None, tq, B_WIDTH), lambda b, i: (b, i, 0)),
            pl.BlockSpec((None, 2 * SEL_BLOCK, tq), lambda b, i: (b, 0, i)),
        ],
        out_shape=[
            jax.ShapeDtypeStruct((bsz, seq, B_WIDTH), F32),
            jax.ShapeDtypeStruct((bsz, 2 * SEL_BLOCK, seq), F32),
        ],
        compiler_params=_cparams(("parallel", "parallel")),
        name="nsa_cmp_select",
    )(p3, kcv, mm, g3)


def _selwin_body(q_ref, ks_ref, vs_ref, kw_ref, vw_ref, sel_ref, ocmp_ref, gate_ref, o_ref,
                 vst_ref, vwt_ref, qs_ref, m_ref, l_ref, acc_ref, *, tq, tk, seq):
    i = pl.program_id(1)
    t0 = i * tq
    nh = B_HEADS // 2
    lanes_q = nh * tq
    lo = _lane_lo()

    @pl.when(i == 0)
    def _():
        step = 512
        for r0 in range(0, seq, step):
            vst_ref[:, r0:r0 + step] = vs_ref[r0:r0 + step, :].astype(F32).T.astype(BF16)
            vwt_ref[:, r0:r0 + step] = vw_ref[r0:r0 + step, :].astype(F32).T.astype(BF16)

    qall = q_ref[...]
    for half in range(2):
        hm = lo if half == 0 else ~lo
        qs_ref[half] = jnp.concatenate(
            [jnp.where(hm, qall[:, c * LANES:(c + 1) * LANES], jnp.zeros((tq, LANES), BF16)) for c in range(nh)], axis=0)

    tq_lane = t0 + lax.broadcasted_iota(jnp.int32, (1, tq), 1)

    def softmax_step(st, bias, vt, m_old):
        sc = st + jnp.concatenate([bias] * nh, axis=1)
        m_new = jnp.max(sc, axis=0, keepdims=True)
        if m_old is not None:
            m_new = jnp.maximum(m_old, m_new)
        p = jnp.exp(sc - m_new)
        return m_new, jnp.sum(p, axis=0, keepdims=True), _dot(vt, p.astype(BF16))

    span = WINDOW + tq
    wstart = pl.multiple_of(jnp.maximum(t0 - WINDOW, 0), LANES)
    dw = tq_lane - (wstart + lax.broadcasted_iota(jnp.int32, (span, 1), 0))
    wbias = jnp.where((dw >= 0) & (dw < WINDOW), 0.0, NEG)
    kw = kw_ref[pl.ds(wstart, span), :]
    o_win = []
    for half in range(2):
        vt = vwt_ref[half * HEAD_DIM:(half + 1) * HEAD_DIM, pl.ds(wstart, span)]
        _, l_w, pv = softmax_step(_dot_nt(kw, qs_ref[half]), wbias, vt, None)
        o_win.append(pv * (1.0 / l_w))

    m_ref[...] = jnp.full(m_ref.shape, NEG, F32)
    l_ref[...] = jnp.zeros(l_ref.shape, F32)
    acc_ref[...] = jnp.zeros(acc_ref.shape, F32)
    nblk = tk // SEL_BLOCK

    def kt_body(kt, carry):
        koff = pl.multiple_of(kt * tk, tk)
        ks = ks_ref[pl.ds(koff, tk), :]
        causal = (koff + lax.broadcasted_iota(jnp.int32, (tk, 1), 0)) <= tq_lane
        for half in range(2):
            flags = sel_ref[pl.ds(pl.multiple_of(half * SEL_BLOCK + kt * nblk, nblk), nblk), :]
            chosen = jnp.concatenate(
                [jnp.broadcast_to(flags[b:b + 1, :], (SEL_BLOCK, tq)) for b in range(nblk)], axis=0)
            bias = jnp.where((chosen > 0.5) & causal, 0.0, NEG)
            vt = vst_ref[half * HEAD_DIM:(half + 1) * HEAD_DIM, pl.ds(koff, tk)]
            m_old = m_ref[half]
            m_new, l_t, pv = softmax_step(_dot_nt(ks, qs_ref[half]), bias, vt, m_old)
            alpha = jnp.exp(m_old - m_new)
            m_ref[half] = m_new
            l_ref[half] = alpha * l_ref[half] + l_t
            acc_ref[half] = alpha * acc_ref[half] + pv
        return carry

    lax.fori_loop(0, (t0 + tq + tk - 1) // tk, kt_body, 0)

    gates = gate_ref[...].astype(F32)
    o_sel = [acc_ref[half] * (1.0 / l_ref[half]) for half in range(2)]
    for c in range(nh):
        cs = slice(c * LANES, (c + 1) * LANES)
        ls = slice(c * tq, (c + 1) * tq)
        osel = jnp.concatenate([o_sel[0][:, ls], o_sel[1][:, ls]], axis=0).T
        owin = jnp.concatenate([o_win[0][:, ls], o_win[1][:, ls]], axis=0).T
        o_ref[:, cs] = (ocmp_ref[:, cs] + _branch_gate(gates, 1, c, lo) * osel
                        + _branch_gate(gates, 2, c, lo) * owin).astype(o_ref.dtype)


def _sel_win(r3, p3, sel, ocmp, g3, *, tq=256, tk=1024):
    bsz, seq, _ = r3.shape
    rows = (B_HEADS // 2) * tq
    full = lambda off: pl.BlockSpec((None, seq, LANES), lambda b, i: (b, 0, off))
    tile = lambda off: pl.BlockSpec((None, tq, B_WIDTH), lambda b, i: (b, i, off))
    return pl.pallas_call(
        functools.partial(_selwin_body, tq=tq, tk=tk, seq=seq),
        grid=(bsz, seq // tq),
        in_specs=[tile(0), full(R_KSLC), full(P_VSLC), full(R_KWIN), full(P_VWIN),
                  pl.BlockSpec((None, 2 * SEL_BLOCK, tq), lambda b, i: (b, 0, i)), tile(0),
                  pl.BlockSpec((None, tq, LANES), lambda b, i: (b, i, G_BRANCH))],
        out_specs=tile(0),
        out_shape=jax.ShapeDtypeStruct((bsz, seq, B_WIDTH), BF16),
        scratch_shapes=[pltpu.VMEM((LANES, seq), BF16)] * 2 + [pltpu.VMEM((2, rows, LANES), BF16)]
        + [pltpu.VMEM((2, 1, rows), F32)] * 2 + [pltpu.VMEM((2, HEAD_DIM, rows), F32)],
        compiler_params=_cparams(("parallel", "arbitrary")),
        name="nsa_sel_win",
    )(r3, r3, p3, r3, p3, sel, ocmp, g3)


def _stick_body(q_ref, k_ref, v_ref, o_ref, carry_ref, acc_ref, *, tq, tk):
    i = pl.program_id(2)
    t0 = i * tq
    q = q_ref[...]
    lo = _lane_lo()
    qh = [jnp.where(lo, q, jnp.zeros_like(q)), jnp.where(lo, jnp.zeros_like(q), q)]
    tpos = t0 + lax.broadcasted_iota(jnp.int32, (tq, 1), 0)
    later = jnp.where(lax.broadcasted_iota(jnp.int32, (tk, tk), 0) > lax.broadcasted_iota(jnp.int32, (tk, tk), 1),
                      1.0, 0.0).astype(BF16)
    carry_ref[...] = jnp.zeros_like(carry_ref)
    acc_ref[...] = jnp.zeros_like(acc_ref)
    ndiag = tq // tk

    def tile(koff, r0, r1, masked):
        k = k_ref[pl.ds(koff, tk), :]
        v = v_ref[pl.ds(koff, tk), :]
        rs = slice(r0, r1)
        if masked:
            before = (koff + lax.broadcasted_iota(jnp.int32, (1, tk), 1)) < tpos[rs]
        for half in range(2):
            z = _dot_nt(qh[half][rs], k)
            sp = jnp.maximum(z, 0.0) + jnp.log(1.0 + jnp.exp(-jnp.abs(z)))
            spm = jnp.where(before, sp, 0.0) if masked else sp
            inside = _dot(spm.astype(BF16), later)
            c = carry_ref[half, rs, :]
            a = jnp.exp((z - sp) + (jnp.concatenate([c] * (tk // LANES), axis=1) - inside))
            if masked:
                a = jnp.where(before, a, 0.0)
            acc_ref[half, rs, :] += _dot(a.astype(BF16), v)
            carry_ref[half, rs, :] = c - jnp.sum(spm, axis=1, keepdims=True)

    for d in reversed(range(ndiag)):
        koff = pl.multiple_of(t0 + d * tk, tk)
        tile(koff, d * tk, (d + 1) * tk, True)
        if (d + 1) * tk < tq:
            tile(koff, (d + 1) * tk, tq, False)

    def body(n, c):
        tile(pl.multiple_of(t0 - (n + 1) * tk, tk), 0, tq, False)
        return c

    lax.fori_loop(0, i * ndiag, body, 0)
    o_ref[...] = jnp.where(lo, acc_ref[0], acc_ref[1]).astype(o_ref.dtype)


def _stick(p3, *, tq=1024, tk=256):
    bsz, seq, _ = p3.shape
    npair = C_HEADS // 2
    full = lambda off: pl.BlockSpec((None, seq, LANES), lambda b, p, i: (b, 0, off + p))
    return pl.pallas_call(
        functools.partial(_stick_body, tq=tq, tk=tk),
        grid=(bsz, npair, seq // tq),
        in_specs=[pl.BlockSpec((None, tq, LANES), lambda b, p, i: (b, i, P_CQ + p)), full(P_CK), full(P_CV)],
        out_specs=pl.BlockSpec((None, tq, LANES), lambda b, p, i: (b, i, p)),
        out_shape=jax.ShapeDtypeStruct((bsz, seq, C_WIDTH), BF16),
        scratch_shapes=[pltpu.VMEM((2, tq, LANES), F32)] * 2,
        compiler_params=_cparams(("parallel", "parallel", "parallel")),
        name="stick_breaking",
    )(p3, p3, p3)


def _merge_body(h_ref, oa_ref, ob_ref, oc_ref, ga_ref, gb_ref, gc_ref, wa_ref, wb_ref, wc_ref, wo_ref, o_ref):
    y = (ga_ref[...].astype(F32) * _dot(oa_ref[...], wa_ref[...])
         + gb_ref[...].astype(F32) * _dot(ob_ref[...], wb_ref[...])
         + gc_ref[...].astype(F32) * _dot(oc_ref[...], wc_ref[...]))
    o_ref[...] = h_ref[...] + _dot(y.astype(BF16), wo_ref[...])


def _merge(h, oa, ob, oc, gates, wa, wb, wc, wo, *, tm=512):
    n, d = h.shape
    row = lambda w: pl.BlockSpec((tm, w), lambda i: (i, 0))
    whole = lambda a: pl.BlockSpec(a.shape, lambda i: (0, 0))
    gate = lambda m: pl.BlockSpec((tm, d), lambda i: (i, m))
    return pl.pallas_call(
        _merge_body,
        grid=(n // tm,),
        in_specs=[row(d), row(A_WIDTH), row(B_WIDTH), row(C_WIDTH), gate(0), gate(1), gate(2),
                  whole(wa), whole(wb), whole(wc), whole(wo)],
        out_specs=row(d),
        out_shape=jax.ShapeDtypeStruct((n, d), F32),
        compiler_params=_cparams(("parallel",)),
        name="merge_out",
    )(h, oa, ob, oc, gates, gates, gates, wa, wb, wc, wo)


def _prep_layer(w_in, w_gate, w_up, pe_k, cw1_k, cw2_k, pe_v, cw1_v, cw2_v):
    b0 = A_IN
    kv0 = b0 + B_WIDTH
    gt0 = kv0 + B_KV_IN
    c0 = b0 + B_IN
    d = w_in.shape[0]
    nh = B_HEADS // 2
    kvcol = lambda br, kvt: kv0 + (br * 2 + kvt) * B_KV * HEAD_DIM
    span = lambda s, w: w_in[:, s:s + w]
    wq_b = (span(b0, B_WIDTH).reshape(d, 2, nh, HEAD_DIM).transpose(0, 2, 1, 3).reshape(d, B_WIDTH) * Q_SCALE).astype(BF16)
    aw = A_GROUPS * A_WIDTH
    w_rope = jnp.concatenate([
        wq_b, (span(0, aw) * Q_SCALE).astype(BF16), span(aw, aw).astype(BF16),
        span(kvcol(1, 0), LANES).astype(BF16), span(kvcol(2, 0), LANES).astype(BF16)], axis=1)
    w_plain = jnp.concatenate([
        wq_b, span(kvcol(0, 0), 2 * LANES).astype(BF16),
        span(kvcol(1, 1), LANES).astype(BF16), span(kvcol(2, 1), LANES).astype(BF16),
        span(2 * aw, aw).astype(BF16), (span(c0, C_WIDTH) * Q_SCALE).astype(BF16),
        span(c0 + C_WIDTH, 2 * C_WIDTH).astype(BF16)], axis=1)
    wg = span(gt0, 3 * B_HEADS).reshape(d, 2, nh, 3).transpose(0, 3, 2, 1).reshape(d, 3 * B_HEADS)
    w_sig = jnp.concatenate([w_gate.astype(BF16), wg.astype(BF16),
                             jnp.zeros((d, G_WIDTH - 3 * D_MODEL - 3 * B_HEADS), BF16)], axis=1)

    def cmp_weights(pe, w1, w2):
        w1r = w1.reshape(2, CMP_STRIDE, HEAD_DIM, CMP_HIDDEN)
        z = jnp.zeros_like(w1r)
        per_head = [jnp.stack([w1r, z], axis=2), jnp.stack([z, w1r], axis=2)]
        wfull = jnp.concatenate(per_head, axis=-1).reshape(2, CMP_STRIDE * LANES, 2 * CMP_HIDDEN)
        zz = jnp.zeros_like(w2)
        w2f = jnp.concatenate([jnp.concatenate([w2, zz], axis=1), jnp.concatenate([zz, w2], axis=1)], axis=0)
        per = pe.reshape(2, CMP_STRIDE, 1, HEAD_DIM)
        pef = jnp.broadcast_to(per, (2, CMP_STRIDE, B_KV, HEAD_DIM)).reshape(2, CMP_STRIDE * LANES)
        return pef, wfull[0].astype(BF16), wfull[1].astype(BF16), w2f.astype(BF16)

    ck, cv = cmp_weights(pe_k, cw1_k, cw2_k), cmp_weights(pe_v, cw1_v, cw2_v)
    cmp_w = tuple(jnp.stack([a, b]) for a, b in zip(ck, cv))
    w_up = w_up.astype(BF16)
    wb = w_up[A_WIDTH:A_WIDTH + B_WIDTH].reshape(2, nh, HEAD_DIM, -1).transpose(1, 0, 2, 3).reshape(B_WIDTH, -1)
    return dict(w_rope=w_rope, w_plain=w_plain, w_sig=w_sig, cmp_w=cmp_w,
                wa=w_up[:A_WIDTH], wb=wb, wc=w_up[A_WIDTH + B_WIDTH:])


def _rope_consts():
    inv = ROPE_THETA ** (-jnp.arange(0, ROT_DIM, 2, dtype=F32) / ROT_DIM)
    lane = np.arange(LANES) % HEAD_DIM
    half = ROT_DIM // 2
    inv_l = jnp.where(lane < ROT_DIM, inv[lane % half], 0.0).reshape(1, LANES).astype(F32)
    sa = np.where(lane < half, -1.0, 0.0).reshape(1, LANES).astype(np.float32)
    sb = np.where((lane >= half) & (lane < ROT_DIM), 1.0, 0.0).reshape(1, LANES).astype(np.float32)
    return inv_l, jnp.asarray(sa), jnp.asarray(sb)


def _block_sum_matrix(nc):
    ratio = SEL_BLOCK // CMP_STRIDE
    per = CMP_BLOCK // CMP_STRIDE
    m = np.zeros((2, nc, LANES), np.float32)
    for j in range(SEL_BLOCK):
        for a in range(ratio):
            for b in range(per):
                c = ratio * j + a + b
                if c < nc - 1:
                    m[0, c, j] += 1.0
                    m[1, c, SEL_BLOCK + j] += 1.0
    return jnp.asarray(m).astype(BF16)


def _mixers(h, lw, norm_mix, rope_in, bsz, seq):
    n = bsz * seq
    r = _proj(h, norm_mix, lw["w_rope"], "rope", rope_in, tn=R_WIDTH // 2)
    p = _proj(h, norm_mix, lw["w_plain"], "plain", tn=P_WIDTH // 2)
    g = _proj(h, norm_mix, lw["w_sig"], "sigmoid", tn=G_WIDTH // 2)
    r3, p3, g3 = r.reshape(bsz, seq, R_WIDTH), p.reshape(bsz, seq, P_WIDTH), g.reshape(bsz, seq, G_WIDTH)
    oa = _attn_a(r3, p3)
    nc = seq // CMP_STRIDE
    xc = p3[:, :, P_KCMP * LANES:(P_KCMP + 2) * LANES].reshape(bsz, nc, CMP_STRIDE, 2, LANES)
    xc = xc.transpose(0, 3, 1, 2, 4).reshape(bsz, 2, nc, CMP_STRIDE * LANES)
    kcv = _compress(xc, *lw["cmp_w"])
    ocmp, sel = _cmp_select(p3, kcv, _block_sum_matrix(nc), g3)
    ob = _sel_win(r3, p3, sel, ocmp, g3)
    oc = _stick(p3)
    return oa.reshape(n, A_WIDTH), ob.reshape(n, B_WIDTH), oc.reshape(n, C_WIDTH), g


def kernel(x, positions, norm_ffn1, ffn1_w1, ffn1_w3, ffn1_w2, norm_mix, w_in, cmp_pe_k, cmp_w1_k, cmp_w2_k,
           cmp_pe_v, cmp_w1_v, cmp_w2_v, w_gate, w_up, w_out, norm_ffn2, ffn2_w1, ffn2_w3, ffn2_w2, norm_final):
    bsz, seq, d = x.shape
    n = bsz * seq
    depth = w_in.shape[0]
    rope_in = (positions.astype(F32).reshape(n, 1),) + _rope_consts()
    h = x.reshape(n, d)
    for i in range(depth):
        lw = _prep_layer(w_in[i], w_gate[i], w_up[i], cmp_pe_k[i], cmp_w1_k[i], cmp_w2_k[i],
                         cmp_pe_v[i], cmp_w1_v[i], cmp_w2_v[i])
        h = _ffn(h, norm_ffn1[i], ffn1_w1[i].astype(BF16), ffn1_w3[i].astype(BF16), ffn1_w2[i].astype(BF16))
        oa, ob, oc, g = _mixers(h, lw, norm_mix[i], rope_in, bsz, seq)
        h = _merge(h, oa, ob, oc, g, lw["wa"], lw["wb"], lw["wc"], w_out[i].astype(BF16))
        h = _ffn(h, norm_ffn2[i], ffn2_w1[i].astype(BF16), ffn2_w3[i].astype(BF16), ffn2_w2[i].astype(BF16),
                 norm_final if i == depth - 1 else None)
    return h.reshape(bsz, seq, d)
```

```python
import functools

import numpy as np
import jax
import jax.numpy as jnp
from jax import lax
from jax.experimental import pallas as pl
from jax.experimental.pallas import tpu as pltpu

F32 = jnp.float32
BF16 = jnp.bfloat16

D_MODEL = 1024
HEAD_DIM = 64
ROT_DIM = HEAD_DIM // 4
ROPE_THETA = 500000.0
NORM_EPS = 1e-6
D_FF = 2816
LANES = 128

DIL_PAIRS = ((128, 1), (512, 4), (2048, 16))
A_GROUPS = 3
A_SLOTS = 6
A_WIDTH = A_SLOTS * HEAD_DIM
B_HEADS = 8
B_KV = 2
B_WIDTH = B_HEADS * HEAD_DIM
CMP_BLOCK = 32
CMP_STRIDE = 16
CMP_HIDDEN = 2 * HEAD_DIM
SEL_BLOCK = 64
N_SELECT = 16
WINDOW = 512
FORCE_SCORE = 1e4
C_HEADS = 6
C_WIDTH = C_HEADS * HEAD_DIM
A_IN = 3 * A_GROUPS * A_WIDTH
B_KV_IN = 3 * 2 * B_KV * HEAD_DIM
B_IN = B_WIDTH + B_KV_IN + 3 * B_HEADS
C_IN = 3 * C_WIDTH
Q_SCALE = HEAD_DIM ** -0.5

NEG = -1e30
VMEM_LIMIT = 48 * 1024 * 1024

R_WIDTH = 3072
R_AQ, R_AK, R_KSLC, R_KWIN = 4, 13, 22, 23
P_WIDTH = 3328
P_KCMP, P_VSLC, P_VWIN, P_AV, P_CQ, P_CK, P_CV = 4, 6, 7, 8, 17, 20, 23
G_WIDTH = 3328
G_BRANCH = 3 * D_MODEL // LANES


def _cparams(sem):
    return pltpu.CompilerParams(dimension_semantics=sem, vmem_limit_bytes=VMEM_LIMIT)


def _rms(x, g):
    return x * lax.rsqrt(jnp.mean(x * x, axis=-1, keepdims=True) + NORM_EPS) * g


def _dot(a, b):
    return jnp.dot(a, b, preferred_element_type=F32)


def _dot_nt(a, b):
    return lax.dot_general(a, b, (((1,), (1,)), ((), ())), preferred_element_type=F32)


def _lane_lo():
    return lax.broadcasted_iota(jnp.int32, (1, LANES), 1) < HEAD_DIM


def _branch_gate(g, branch, c, lo):
    l0 = branch * B_HEADS + 2 * c
    return jnp.where(lo, g[:, l0:l0 + 1], g[:, l0 + 1:l0 + 2])


def _ffn_body(*refs, chunks, final):
    if final:
        x_ref, g_ref, w1_ref, w3_ref, w2_ref, gf_ref, o_ref, acc_ref = refs
    else:
        x_ref, g_ref, w1_ref, w3_ref, w2_ref, o_ref, acc_ref = refs
    xn = _rms(x_ref[...], g_ref[...]).astype(BF16)
    f0 = 0
    for n, fc in enumerate(chunks):
        a = _dot(xn, w1_ref[:, f0:f0 + fc])
        b = _dot(xn, w3_ref[:, f0:f0 + fc])
        part = _dot((a * jax.nn.sigmoid(a) * b).astype(BF16), w2_ref[f0:f0 + fc, :])
        if n == 0:
            acc_ref[...] = part
        else:
            acc_ref[...] += part
        f0 += fc
    y = x_ref[...] + 0.5 * acc_ref[...]
    if final:
        y = _rms(y, gf_ref[...])
    o_ref[...] = y


def _resident(shape):
    return pl.BlockSpec(shape, lambda i: (0,) * len(shape), pipeline_mode=pl.Buffered(1))


def _ffn(x, g, w1, w3, w2, g_final=None, *, tm=512, chunks=(768, 768, 768, 512)):
    n, d = x.shape
    assert sum(chunks) == D_FF
    final = g_final is not None
    in_specs = [pl.BlockSpec((tm, d), lambda i: (i, 0)), _resident((1, d)),
                _resident(w1.shape), _resident(w3.shape), _resident(w2.shape)]
    args = [x, g.reshape(1, d), w1, w3, w2]
    if final:
        in_specs.append(_resident((1, d)))
        args.append(g_final.reshape(1, d))
    return pl.pallas_call(
        functools.partial(_ffn_body, chunks=chunks, final=final),
        grid=(n // tm,),
        in_specs=in_specs,
        out_specs=pl.BlockSpec((tm, d), lambda i: (i, 0)),
        out_shape=jax.ShapeDtypeStruct((n, d), F32),
        scratch_shapes=[pltpu.VMEM((tm, d), F32)],
        compiler_params=_cparams(("parallel",)),
        name="ffn",
    )(*args)


def _col_chunks(width, step):
    return [(c0, min(step, width - c0)) for c0 in range(0, width, step)]


def _proj_body(x_ref, g_ref, pos_ref, inv_ref, sa_ref, sb_ref, wr_ref, wp_ref, ws_ref, r_ref, p_ref, s_ref, *, step):
    xn = _rms(x_ref[...], g_ref[...]).astype(BF16)
    ang = pos_ref[...] * inv_ref[...]
    cos = jnp.cos(ang)
    sin = jnp.sin(ang)
    sina = sin * sa_ref[...]
    sinb = sin * sb_ref[...]
    half = ROT_DIM // 2
    for c0, w in _col_chunks(R_WIDTH, step):
        acc = _dot(xn, wr_ref[:, c0:c0 + w])
        for c in range(w // LANES):
            y = acc[:, c * LANES:(c + 1) * LANES]
            up = pltpu.roll(y, LANES - half, axis=1)
            dn = pltpu.roll(y, half, axis=1)
            r_ref[:, c0 + c * LANES:c0 + (c + 1) * LANES] = (y * cos + up * sina + dn * sinb).astype(r_ref.dtype)
    for c0, w in _col_chunks(P_WIDTH, step):
        p_ref[:, c0:c0 + w] = _dot(xn, wp_ref[:, c0:c0 + w]).astype(p_ref.dtype)
    for c0, w in _col_chunks(G_WIDTH, step):
        s_ref[:, c0:c0 + w] = jax.nn.sigmoid(_dot(xn, ws_ref[:, c0:c0 + w])).astype(s_ref.dtype)


def _proj(h, g, w_rope, w_plain, w_sig, rope_in, *, tm=256, step=768):
    n, d = h.shape
    posf, inv, sa, sb = rope_in
    row = lambda w: pl.BlockSpec((tm, w), lambda i: (i, 0))
    return pl.pallas_call(
        functools.partial(_proj_body, step=step),
        grid=(n // tm,),
        in_specs=[row(d), _resident((1, d)), row(1)] + [_resident((1, LANES))] * 3
        + [_resident(w_rope.shape), _resident(w_plain.shape), _resident(w_sig.shape)],
        out_specs=[row(R_WIDTH), row(P_WIDTH), row(G_WIDTH)],
        out_shape=[jax.ShapeDtypeStruct((n, w), BF16) for w in (R_WIDTH, P_WIDTH, G_WIDTH)],
        compiler_params=_cparams(("parallel",)),
        name="proj",
    )(h, g.reshape(1, d), posf, inv, sa, sb, w_rope, w_plain, w_sig)


def _attn_a_body(q_ref, k_ref, v_ref, o_ref, qf, kf, vf, m_s, l_s, acc_s, *, seq):
    g = pl.program_id(2)
    qf[...] = q_ref[...].astype(F32)
    kf[...] = k_ref[...].astype(F32)
    vf[...] = v_ref[...].astype(F32)
    lo = _lane_lo()
    blk = LANES
    qi = lax.broadcasted_iota(jnp.int32, (blk, 1), 0) + blk
    ki = lax.broadcasted_iota(jnp.int32, (1, 2 * blk), 1)
    dist = qi - ki

    def run_group(window, dil, first):
        steps = window // dil
        per = seq // (blk * dil)
        in_band = (dist >= 0) & (dist <= steps)

        def rows(ref, start):
            if dil == 1:
                return ref[pl.ds(pl.multiple_of(start, blk), blk), :]
            return ref[pl.ds(start, blk, stride=dil), :]

        ub = min(4, per)

        def body(it, carry):
            r = (it * ub) // per
            j0 = it * ub - r * per
            base = r + dil * blk * j0
            prev = jnp.maximum(base - dil * blk, r)
            kbs = [rows(kf, prev).astype(BF16)] + [rows(kf, base + u * dil * blk).astype(BF16) for u in range(ub)]
            vbs = [rows(vf, prev).astype(BF16)] + [rows(vf, base + u * dil * blk).astype(BF16) for u in range(ub)]
            for u in range(ub):
                block(base + u * dil * blk, jnp.concatenate(kbs[u:u + 2], axis=0),
                      jnp.concatenate(vbs[u:u + 2], axis=0), in_band & ((ki >= blk) | (j0 + u > 0)))
            return carry

        def block(start, kcat, vcat, valid):
            qb = rows(qf, start)
            ms, ls, os_ = [], [], []
            for half in range(2):
                qm = jnp.where(lo if half == 0 else ~lo, qb, 0.0).astype(BF16)
                s = jnp.where(valid, _dot_nt(qm, kcat), NEG)
                m = jnp.max(s, axis=1, keepdims=True)
                p = jnp.exp(s - m)
                ls.append(jnp.sum(p, axis=1, keepdims=True))
                ms.append(m)
                os_.append(_dot(p.astype(BF16), vcat))
            m_b = jnp.where(lo, ms[0], ms[1])
            l_b = jnp.where(lo, ls[0], ls[1])
            o_b = jnp.where(lo, os_[0], os_[1])
            sl = (pl.ds(pl.multiple_of(start, blk), blk) if dil == 1 else pl.ds(start, blk, stride=dil), slice(None))
            if first:
                m_s[sl] = m_b
                l_s[sl] = l_b
                acc_s[sl] = o_b
            else:
                m_old = m_s[sl]
                m_new = jnp.maximum(m_old, m_b)
                a_old = jnp.exp(m_old - m_new)
                a_b = jnp.exp(m_b - m_new)
                m_s[sl] = m_new
                l_s[sl] = l_s[sl] * a_old + l_b * a_b
                acc_s[sl] = acc_s[sl] * a_old + o_b * a_b

        lax.fori_loop(0, seq // (blk * ub), body, 0, unroll=8 // ub)

    for step, (window, dil) in enumerate(reversed(DIL_PAIRS)):
        pl.when(g == step)(functools.partial(run_group, window, dil, step == 0))

    @pl.when(g == A_GROUPS - 1)
    def _():
        o_ref[...] = (acc_s[...] / l_s[...]).astype(o_ref.dtype)


def _attn_a(r3, p3):
    bsz, seq, _ = r3.shape
    npair = A_SLOTS // 2
    blk = lambda off: pl.BlockSpec((None, seq, LANES), lambda b, p, g: (b, 0, off + (A_GROUPS - 1 - g) * npair + p))
    return pl.pallas_call(
        functools.partial(_attn_a_body, seq=seq),
        grid=(bsz, npair, A_GROUPS),
        in_specs=[blk(R_AQ), blk(R_AK), blk(P_AV)],
        out_specs=pl.BlockSpec((None, seq, LANES), lambda b, p, g: (b, 0, p)),
        out_shape=jax.ShapeDtypeStruct((bsz, seq, A_WIDTH), BF16),
        scratch_shapes=[pltpu.VMEM((seq, LANES), F32)] * 6,
        compiler_params=_cparams(("parallel", "parallel", "arbitrary")),
        name="attn_dilated",
    )(r3, r3, p3)


def _compress_body(x_ref, pe_ref, wlo_ref, whi_ref, w2_ref, o_ref, *, nc):
    x = x_ref[...].astype(F32)
    ylo = _dot((x + pe_ref[0:1, :]).astype(BF16), wlo_ref[...])
    yhi = _dot((x + pe_ref[1:2, :]).astype(BF16), whi_ref[...])
    pre = ylo + pltpu.roll(yhi, nc - 1, axis=0)
    o_ref[...] = _dot(jax.nn.gelu(pre).astype(BF16), w2_ref[...]).astype(o_ref.dtype)


def _compress(xc, pe, wlo, whi, w2):
    bsz, _, nc, width = xc.shape
    hid = wlo.shape[-1]
    return pl.pallas_call(
        functools.partial(_compress_body, nc=nc),
        grid=(bsz, 2),
        in_specs=[
            pl.BlockSpec((None, None, nc, width), lambda b, t: (b, t, 0, 0)),
            pl.BlockSpec((None, 2, width), lambda b, t: (t, 0, 0)),
            pl.BlockSpec((None, width, hid), lambda b, t: (t, 0, 0)),
            pl.BlockSpec((None, width, hid), lambda b, t: (t, 0, 0)),
            pl.BlockSpec((None, hid, LANES), lambda b, t: (t, 0, 0)),
        ],
        out_specs=pl.BlockSpec((None, None, nc, LANES), lambda b, t: (b, t, 0, 0)),
        out_shape=jax.ShapeDtypeStruct((bsz, 2, nc, LANES), BF16),
        compiler_params=_cparams(("parallel", "parallel")),
        name="nsa_compress",
    )(xc, pe, wlo, whi, w2)


def _split3(x):
    hi = x.astype(BF16)
    r1 = x - hi.astype(F32)
    mid = r1.astype(BF16)
    lo = (r1 - mid.astype(F32)).astype(BF16)
    return hi, mid, lo


def _cmp_body(q_ref, kcv_ref, mm_ref, gate_ref, ocmp_ref, sel_ref, *, tq, nc):
    t0 = pl.program_id(1) * tq
    q = q_ref[...]
    kc = kcv_ref[0]
    vc = kcv_ref[1]
    lo = _lane_lo()
    t = t0 + lax.broadcasted_iota(jnp.int32, (tq, 1), 0)
    cend = lax.broadcasted_iota(jnp.int32, (1, nc), 1) * CMP_STRIDE + (CMP_BLOCK - 1)
    valid = cend <= t
    gates = gate_ref[...].astype(F32)
    imp = [jnp.zeros((tq, nc), F32), jnp.zeros((tq, nc), F32)]
    for c in range(B_HEADS // 2):
        qc = q[:, c * LANES:(c + 1) * LANES]
        outs = []
        for half in range(2):
            qm = jnp.where(lo if half == 0 else ~lo, qc, jnp.zeros_like(qc))
            s = jnp.where(valid, _dot_nt(qm, kc), NEG)
            m = jnp.max(s, axis=1, keepdims=True)
            p = jnp.where(valid, jnp.exp(s - m), 0.0)
            l = jnp.sum(p, axis=1, keepdims=True)
            pn = p * (1.0 / jnp.where(l > 0, l, 1.0))
            imp[half] = imp[half] + pn
            outs.append(_dot(pn.astype(BF16), vc))
        oc = jnp.where(lo, outs[0], outs[1])
        ocmp_ref[:, c * LANES:(c + 1) * LANES] = oc * _branch_gate(gates, 0, c, lo)

    score = jnp.zeros((tq, LANES), F32)
    for half in range(2):
        for part in _split3(imp[half]):
            score = score + _dot(part, mm_ref[half])
    st = score.T
    row = lax.broadcasted_iota(jnp.int32, (LANES, 1), 0)
    jb = jnp.where(row < SEL_BLOCK, row, row - SEL_BLOCK)
    tl = t0 + lax.broadcasted_iota(jnp.int32, (1, tq), 1)
    cur = tl // SEL_BLOCK
    forced = (jb == 0) | (jb == cur) | (jb == cur - 1)
    st = jnp.where(forced, FORCE_SCORE, jnp.where(jb * SEL_BLOCK <= tl, st, -1.0))
    sub = 8
    sub_row = lax.broadcasted_iota(jnp.int32, (sub, 1), 0)
    picked = []
    for half in range(2):
        chunks = [st[half * SEL_BLOCK + k * sub:half * SEL_BLOCK + (k + 1) * sub] for k in range(SEL_BLOCK // sub)]
        ranks = [jnp.zeros((sub, tq), F32) for _ in chunks]
        for jp in range(SEL_BLOCK):
            other = chunks[jp // sub][jp % sub:jp % sub + 1]
            for k, ch in enumerate(chunks):
                if k < jp // sub:
                    beats = other > ch
                elif k > jp // sub:
                    beats = other >= ch
                else:
                    beats = (other > ch) | ((other == ch) & (sub_row > jp % sub))
                ranks[k] = ranks[k] + jnp.where(beats, 1.0, 0.0)
        picked += [jnp.where(r < N_SELECT, 1.0, 0.0) for r in ranks]
    sel_ref[...] = jnp.concatenate(picked, axis=0)


def _cmp_select(p3, kcv, mm, g3, *, tq=256):
    bsz, seq, _ = p3.shape
    nc = kcv.shape[2]
    return pl.pallas_call(
        functools.partial(_cmp_body, tq=tq, nc=nc),
        grid=(bsz, seq // tq),
        in_specs=[
            pl.BlockSpec((None, tq, B_WIDTH), lambda b, i: (b, i, 0)),
            pl.BlockSpec((None, 2, nc, LANES), lambda b, i: (b, 0, 0, 0)),
            pl.BlockSpec((2, nc, LANES), lambda b, i: (0, 0, 0)),
            pl.BlockSpec((None, tq, LANES), lambda b, i: (b, i, G_BRANCH)),
        ],
        out_specs=[
            pl.BlockSpec((None, tq, B_WIDTH), lambda b, i: (b, i, 0)),
            pl.BlockSpec((None, 2 * SEL_BLOCK, tq), lambda b, i: (b, 0, i)),
        ],
        out_shape=[
            jax.ShapeDtypeStruct((bsz, seq, B_WIDTH), F32),
            jax.ShapeDtypeStruct((bsz, 2 * SEL_BLOCK, seq), F32),
        ],
        compiler_params=_cparams(("parallel", "parallel")),
        name="nsa_cmp_select",
    )(p3, kcv, mm, g3)


def _selwin_body(q_ref, ks_ref, vs_ref, kw_ref, vw_ref, sel_ref, ocmp_ref, gate_ref, o_ref,
                 vst_ref, vwt_ref, qs_ref, m_ref, l_ref, acc_ref, *, tq, tk, seq):
    i = pl.program_id(1)
    t0 = i * tq
    nh = B_HEADS // 2
    lanes_q = nh * tq
    lo = _lane_lo()

    @pl.when(i == 0)
    def _():
        step = 512
        for r0 in range(0, seq, step):
            vst_ref[:, r0:r0 + step] = vs_ref[r0:r0 + step, :].astype(F32).T.astype(BF16)
            vwt_ref[:, r0:r0 + step] = vw_ref[r0:r0 + step, :].astype(F32).T.astype(BF16)

    qall = q_ref[...]
    for half in range(2):
        hm = lo if half == 0 else ~lo
        qs_ref[half] = jnp.concatenate(
            [jnp.where(hm, qall[:, c * LANES:(c + 1) * LANES], jnp.zeros((tq, LANES), BF16)) for c in range(nh)], axis=0)

    tq_lane = t0 + lax.broadcasted_iota(jnp.int32, (1, tq), 1)

    def softmax_step(st, bias, vt, m_old):
        sc = st + jnp.concatenate([bias] * nh, axis=1)
        m_new = jnp.max(sc, axis=0, keepdims=True)
        if m_old is not None:
            m_new = jnp.maximum(m_old, m_new)
        p = jnp.exp(sc - m_new)
        return m_new, jnp.sum(p, axis=0, keepdims=True), _dot(vt, p.astype(BF16))

    span = WINDOW + tq
    wstart = pl.multiple_of(jnp.maximum(t0 - WINDOW, 0), LANES)
    dw = tq_lane - (wstart + lax.broadcasted_iota(jnp.int32, (span, 1), 0))
    wbias = jnp.where((dw >= 0) & (dw < WINDOW), 0.0, NEG)
    kw = kw_ref[pl.ds(wstart, span), :]
    o_win = []
    for half in range(2):
        vt = vwt_ref[half * HEAD_DIM:(half + 1) * HEAD_DIM, pl.ds(wstart, span)]
        _, l_w, pv = softmax_step(_dot_nt(kw, qs_ref[half]), wbias, vt, None)
        o_win.append(pv * (1.0 / l_w))

    m_ref[...] = jnp.full(m_ref.shape, NEG, F32)
    l_ref[...] = jnp.zeros(l_ref.shape, F32)
    acc_ref[...] = jnp.zeros(acc_ref.shape, F32)
    nblk = tk // SEL_BLOCK

    def kt_body(kt, carry):
        koff = pl.multiple_of(kt * tk, tk)
        ks = ks_ref[pl.ds(koff, tk), :]
        causal = (koff + lax.broadcasted_iota(jnp.int32, (tk, 1), 0)) <= tq_lane
        for half in range(2):
            flags = sel_ref[pl.ds(pl.multiple_of(half * SEL_BLOCK + kt * nblk, nblk), nblk), :]
            chosen = jnp.concatenate(
                [jnp.broadcast_to(flags[b:b + 1, :], (SEL_BLOCK, tq)) for b in range(nblk)], axis=0)
            bias = jnp.where((chosen > 0.5) & causal, 0.0, NEG)
            vt = vst_ref[half * HEAD_DIM:(half + 1) * HEAD_DIM, pl.ds(koff, tk)]
            m_old = m_ref[half]
            m_new, l_t, pv = softmax_step(_dot_nt(ks, qs_ref[half]), bias, vt, m_old)
            alpha = jnp.exp(m_old - m_new)
            m_ref[half] = m_new
            l_ref[half] = alpha * l_ref[half] + l_t
            acc_ref[half] = alpha * acc_ref[half] + pv
        return carry

    lax.fori_loop(0, (t0 + tq + tk - 1) // tk, kt_body, 0)

    gates = gate_ref[...].astype(F32)
    o_sel = [acc_ref[half] * (1.0 / l_ref[half]) for half in range(2)]
    for c in range(nh):
        cs = slice(c * LANES, (c + 1) * LANES)
        ls = slice(c * tq, (c + 1) * tq)
        osel = jnp.concatenate([o_sel[0][:, ls], o_sel[1][:, ls]], axis=0).T
        owin = jnp.concatenate([o_win[0][:, ls], o_win[1][:, ls]], axis=0).T
        o_ref[:, cs] = (ocmp_ref[:, cs] + _branch_gate(gates, 1, c, lo) * osel
                        + _branch_gate(gates, 2, c, lo) * owin).astype(o_ref.dtype)


def _sel_win(r3, p3, sel, ocmp, g3, *, tq=256, tk=1024):
    bsz, seq, _ = r3.shape
    rows = (B_HEADS // 2) * tq
    full = lambda off: pl.BlockSpec((None, seq, LANES), lambda b, i: (b, 0, off))
    tile = lambda off: pl.BlockSpec((None, tq, B_WIDTH), lambda b, i: (b, i, off))
    return pl.pallas_call(
        functools.partial(_selwin_body, tq=tq, tk=tk, seq=seq),
        grid=(bsz, seq // tq),
        in_specs=[tile(0), full(R_KSLC), full(P_VSLC), full(R_KWIN), full(P_VWIN),
                  pl.BlockSpec((None, 2 * SEL_BLOCK, tq), lambda b, i: (b, 0, i)), tile(0),
                  pl.BlockSpec((None, tq, LANES), lambda b, i: (b, i, G_BRANCH))],
        out_specs=tile(0),
        out_shape=jax.ShapeDtypeStruct((bsz, seq, B_WIDTH), BF16),
        scratch_shapes=[pltpu.VMEM((LANES, seq), BF16)] * 2 + [pltpu.VMEM((2, rows, LANES), BF16)]
        + [pltpu.VMEM((2, 1, rows), F32)] * 2 + [pltpu.VMEM((2, HEAD_DIM, rows), F32)],
        compiler_params=_cparams(("parallel", "arbitrary")),
        name="nsa_sel_win",
    )(r3, r3, p3, r3, p3, sel, ocmp, g3)


def _stick_body(q_ref, k_ref, v_ref, o_ref, carry_ref, acc_ref, *, tq, tk):
    i = pl.program_id(2)
    t0 = i * tq
    q = q_ref[...]
    lo = _lane_lo()
    qh = [jnp.where(lo, q, jnp.zeros_like(q)), jnp.where(lo, jnp.zeros_like(q), q)]
    tpos = t0 + lax.broadcasted_iota(jnp.int32, (tq, 1), 0)
    later = jnp.where(lax.broadcasted_iota(jnp.int32, (tk, tk), 0) > lax.broadcasted_iota(jnp.int32, (tk, tk), 1),
                      1.0, 0.0).astype(BF16)
    carry_ref[...] = jnp.zeros_like(carry_ref)
    acc_ref[...] = jnp.zeros_like(acc_ref)
    ndiag = tq // tk

    def tile(koff, r0, r1, masked):
        k = k_ref[pl.ds(koff, tk), :]
        v = v_ref[pl.ds(koff, tk), :]
        rs = slice(r0, r1)
        if masked:
            before = (koff + lax.broadcasted_iota(jnp.int32, (1, tk), 1)) < tpos[rs]
        for half in range(2):
            z = _dot_nt(qh[half][rs], k)
            sp = jnp.maximum(z, 0.0) + jnp.log(1.0 + jnp.exp(-jnp.abs(z)))
            spm = jnp.where(before, sp, 0.0) if masked else sp
            inside = _dot(spm.astype(BF16), later)
            c = carry_ref[half, rs, :]
            a = jnp.exp((z - sp) + (jnp.concatenate([c] * (tk // LANES), axis=1) - inside))
            if masked:
                a = jnp.where(before, a, 0.0)
            acc_ref[half, rs, :] += _dot(a.astype(BF16), v)
            carry_ref[half, rs, :] = c - jnp.sum(spm, axis=1, keepdims=True)

    for d in reversed(range(ndiag)):
        koff = pl.multiple_of(t0 + d * tk, tk)
        tile(koff, d * tk, (d + 1) * tk, True)
        if (d + 1) * tk < tq:
            tile(koff, (d + 1) * tk, tq, False)

    def body(n, c):
        tile(pl.multiple_of(t0 - (n + 1) * tk, tk), 0, tq, False)
        return c

    lax.fori_loop(0, i * ndiag, body, 0)
    o_ref[...] = jnp.where(lo, acc_ref[0], acc_ref[1]).astype(o_ref.dtype)


def _stick(p3, *, tq=1024, tk=256):
    bsz, seq, _ = p3.shape
    npair = C_HEADS // 2
    full = lambda off: pl.BlockSpec((None, seq, LANES), lambda b, p, i: (b, 0, off + p))
    return pl.pallas_call(
        functools.partial(_stick_body, tq=tq, tk=tk),
        grid=(bsz, npair, seq // tq),
        in_specs=[pl.BlockSpec((None, tq, LANES), lambda b, p, i: (b, i, P_CQ + p)), full(P_CK), full(P_CV)],
        out_specs=pl.BlockSpec((None, tq, LANES), lambda b, p, i: (b, i, p)),
        out_shape=jax.ShapeDtypeStruct((bsz, seq, C_WIDTH), BF16),
        scratch_shapes=[pltpu.VMEM((2, tq, LANES), F32)] * 2,
        compiler_params=_cparams(("parallel", "parallel", "parallel")),
        name="stick_breaking",
    )(p3, p3, p3)


def _merge_body(h_ref, oa_ref, ob_ref, oc_ref, ga_ref, gb_ref, gc_ref, wa_ref, wb_ref, wc_ref, wo_ref, o_ref):
    y = (ga_ref[...].astype(F32) * _dot(oa_ref[...], wa_ref[...])
         + gb_ref[...].astype(F32) * _dot(ob_ref[...], wb_ref[...])
         + gc_ref[...].astype(F32) * _dot(oc_ref[...], wc_ref[...]))
    o_ref[...] = h_ref[...] + _dot(y.astype(BF16), wo_ref[...])


def _merge(h, oa, ob, oc, gates, wa, wb, wc, wo, *, tm=512):
    n, d = h.shape
    row = lambda w: pl.BlockSpec((tm, w), lambda i: (i, 0))
    whole = lambda a: pl.BlockSpec(a.shape, lambda i: (0, 0))
    gate = lambda m: pl.BlockSpec((tm, d), lambda i: (i, m))
    return pl.pallas_call(
        _merge_body,
        grid=(n // tm,),
        in_specs=[row(d), row(A_WIDTH), row(B_WIDTH), row(C_WIDTH), gate(0), gate(1), gate(2),
                  whole(wa), whole(wb), whole(wc), whole(wo)],
        out_specs=row(d),
        out_shape=jax.ShapeDtypeStruct((n, d), F32),
        compiler_params=_cparams(("parallel",)),
        name="merge_out",
    )(h, oa, ob, oc, gates, gates, gates, wa, wb, wc, wo)


def _prep_layer(w_in, w_gate, w_up, pe_k, cw1_k, cw2_k, pe_v, cw1_v, cw2_v):
    b0 = A_IN
    kv0 = b0 + B_WIDTH
    gt0 = kv0 + B_KV_IN
    c0 = b0 + B_IN
    d = w_in.shape[0]
    nh = B_HEADS // 2
    kvcol = lambda br, kvt: kv0 + (br * 2 + kvt) * B_KV * HEAD_DIM
    span = lambda s, w: w_in[:, s:s + w]
    wq_b = (span(b0, B_WIDTH).reshape(d, 2, nh, HEAD_DIM).transpose(0, 2, 1, 3).reshape(d, B_WIDTH) * Q_SCALE).astype(BF16)
    aw = A_GROUPS * A_WIDTH
    w_rope = jnp.concatenate([
        wq_b, (span(0, aw) * Q_SCALE).astype(BF16), span(aw, aw).astype(BF16),
        span(kvcol(1, 0), LANES).astype(BF16), span(kvcol(2, 0), LANES).astype(BF16)], axis=1)
    w_plain = jnp.concatenate([
        wq_b, span(kvcol(0, 0), 2 * LANES).astype(BF16),
        span(kvcol(1, 1), LANES).astype(BF16), span(kvcol(2, 1), LANES).astype(BF16),
        span(2 * aw, aw).astype(BF16), (span(c0, C_WIDTH) * Q_SCALE).astype(BF16),
        span(c0 + C_WIDTH, 2 * C_WIDTH).astype(BF16)], axis=1)
    wg = span(gt0, 3 * B_HEADS).reshape(d, 2, nh, 3).transpose(0, 3, 2, 1).reshape(d, 3 * B_HEADS)
    w_sig = jnp.concatenate([w_gate.astype(BF16), wg.astype(BF16),
                             jnp.zeros((d, G_WIDTH - 3 * D_MODEL - 3 * B_HEADS), BF16)], axis=1)

    def cmp_weights(pe, w1, w2):
        w1r = w1.reshape(2, CMP_STRIDE, HEAD_DIM, CMP_HIDDEN)
        z = jnp.zeros_like(w1r)
        per_head = [jnp.stack([w1r, z], axis=2), jnp.stack([z, w1r], axis=2)]
        wfull = jnp.concatenate(per_head, axis=-1).reshape(2, CMP_STRIDE * LANES, 2 * CMP_HIDDEN)
        zz = jnp.zeros_like(w2)
        w2f = jnp.concatenate([jnp.concatenate([w2, zz], axis=1), jnp.concatenate([zz, w2], axis=1)], axis=0)
        per = pe.reshape(2, CMP_STRIDE, 1, HEAD_DIM)
        pef = jnp.broadcast_to(per, (2, CMP_STRIDE, B_KV, HEAD_DIM)).reshape(2, CMP_STRIDE * LANES)
        return pef, wfull[0].astype(BF16), wfull[1].astype(BF16), w2f.astype(BF16)

    ck, cv = cmp_weights(pe_k, cw1_k, cw2_k), cmp_weights(pe_v, cw1_v, cw2_v)
    cmp_w = tuple(jnp.stack([a, b]) for a, b in zip(ck, cv))
    w_up = w_up.astype(BF16)
    wb = w_up[A_WIDTH:A_WIDTH + B_WIDTH].reshape(2, nh, HEAD_DIM, -1).transpose(1, 0, 2, 3).reshape(B_WIDTH, -1)
    return dict(w_rope=w_rope, w_plain=w_plain, w_sig=w_sig, cmp_w=cmp_w,
                wa=w_up[:A_WIDTH], wb=wb, wc=w_up[A_WIDTH + B_WIDTH:])


def _rope_consts():
    inv = ROPE_THETA ** (-jnp.arange(0, ROT_DIM, 2, dtype=F32) / ROT_DIM)
    lane = np.arange(LANES) % HEAD_DIM
    half = ROT_DIM // 2
    inv_l = jnp.where(lane < ROT_DIM, inv[lane % half], 0.0).reshape(1, LANES).astype(F32)
    sa = np.where(lane < half, -1.0, 0.0).reshape(1, LANES).astype(np.float32)
    sb = np.where((lane >= half) & (lane < ROT_DIM), 1.0, 0.0).reshape(1, LANES).astype(np.float32)
    return inv_l, jnp.asarray(sa), jnp.asarray(sb)


def _block_sum_matrix(nc):
    ratio = SEL_BLOCK // CMP_STRIDE
    per = CMP_BLOCK // CMP_STRIDE
    m = np.zeros((2, nc, LANES), np.float32)
    for j in range(SEL_BLOCK):
        for a in range(ratio):
            for b in range(per):
                c = ratio * j + a + b
                if c < nc - 1:
                    m[0, c, j] += 1.0
                    m[1, c, SEL_BLOCK + j] += 1.0
    return jnp.asarray(m).astype(BF16)


def _mixers(h, lw, norm_mix, rope_in, bsz, seq):
    n = bsz * seq
    r, p, g = _proj(h, norm_mix, lw["w_rope"], lw["w_plain"], lw["w_sig"], rope_in)
    r3, p3, g3 = r.reshape(bsz, seq, R_WIDTH), p.reshape(bsz, seq, P_WIDTH), g.reshape(bsz, seq, G_WIDTH)
    oa = _attn_a(r3, p3)
    nc = seq // CMP_STRIDE
    xc = p3[:, :, P_KCMP * LANES:(P_KCMP + 2) * LANES].reshape(bsz, nc, CMP_STRIDE, 2, LANES)
    xc = xc.transpose(0, 3, 1, 2, 4).reshape(bsz, 2, nc, CMP_STRIDE * LANES)
    kcv = _compress(xc, *lw["cmp_w"])
    ocmp, sel = _cmp_select(p3, kcv, _block_sum_matrix(nc), g3)
    ob = _sel_win(r3, p3, sel, ocmp, g3)
    oc = _stick(p3)
    return oa.reshape(n, A_WIDTH), ob.reshape(n, B_WIDTH), oc.reshape(n, C_WIDTH), g


def kernel(x, positions, norm_ffn1, ffn1_w1, ffn1_w3, ffn1_w2, norm_mix, w_in, cmp_pe_k, cmp_w1_k, cmp_w2_k,
           cmp_pe_v, cmp_w1_v, cmp_w2_v, w_gate, w_up, w_out, norm_ffn2, ffn2_w1, ffn2_w3, ffn2_w2, norm_final):
    bsz, seq, d = x.shape
    n = bsz * seq
    depth = w_in.shape[0]
    rope_in = (positions.astype(F32).reshape(n, 1),) + _rope_consts()
    h = x.reshape(n, d)
    for i in range(depth):
        lw = _prep_layer(w_in[i], w_gate[i], w_up[i], cmp_pe_k[i], cmp_w1_k[i], cmp_w2_k[i],
                         cmp_pe_v[i], cmp_w1_v[i], cmp_w2_v[i])
        h = _ffn(h, norm_ffn1[i], ffn1_w1[i].astype(BF16), ffn1_w3[i].astype(BF16), ffn1_w2[i].astype(BF16))
        oa, ob, oc, g = _mixers(h, lw, norm_mix[i], rope_in, bsz, seq)
        h = _merge(h, oa, ob, oc, g, lw["wa"], lw["wb"], lw["wc"], w_out[i].astype(BF16))
        h = _ffn(h, norm_ffn2[i], ffn2_w1[i].astype(BF16), ffn2_w3[i].astype(BF16), ffn2_w2[i].astype(BF16),
                 norm_final if i == depth - 1 else None)
    return h.reshape(bsz, seq, d)
```

```python
import functools

import numpy as np
import jax
import jax.numpy as jnp
from jax import lax
from jax.experimental import pallas as pl
from jax.experimental.pallas import tpu as pltpu

F32 = jnp.float32
BF16 = jnp.bfloat16

D_MODEL = 1024
HEAD_DIM = 64
ROT_DIM = HEAD_DIM // 4
ROPE_THETA = 500000.0
NORM_EPS = 1e-6
D_FF = 2816
LANES = 128

DIL_PAIRS = ((128, 1), (512, 4), (2048, 16))
A_GROUPS = 3
A_SLOTS = 6
A_WIDTH = A_SLOTS * HEAD_DIM
B_HEADS = 8
B_KV = 2
B_WIDTH = B_HEADS * HEAD_DIM
CMP_BLOCK = 32
CMP_STRIDE = 16
CMP_HIDDEN = 2 * HEAD_DIM
SEL_BLOCK = 64
N_SELECT = 16
WINDOW = 512
FORCE_SCORE = 1e4
C_HEADS = 6
C_WIDTH = C_HEADS * HEAD_DIM
A_IN = 3 * A_GROUPS * A_WIDTH
B_KV_IN = 3 * 2 * B_KV * HEAD_DIM
B_IN = B_WIDTH + B_KV_IN + 3 * B_HEADS
C_IN = 3 * C_WIDTH
Q_SCALE = HEAD_DIM ** -0.5
Q_SCALE2 = Q_SCALE * float(np.log2(np.e))

NEG = -1e30
ONES_ROWS = 16
VMEM_LIMIT = 48 * 1024 * 1024

R_WIDTH = 3072
R_AQ, R_AK, R_KSLC, R_KWIN = 4, 13, 22, 23
P_WIDTH = 3328
P_KCMP, P_VSLC, P_VWIN, P_AV, P_CQ, P_CK, P_CV = 4, 6, 7, 8, 17, 20, 23
G_WIDTH = 3328
G_BRANCH = 3 * D_MODEL // LANES


def _cparams(sem):
    return pltpu.CompilerParams(dimension_semantics=sem, vmem_limit_bytes=VMEM_LIMIT)


def _rms(x, g):
    return x * lax.rsqrt(jnp.mean(x * x, axis=-1, keepdims=True) + NORM_EPS) * g


def _dot(a, b):
    return jnp.dot(a, b, preferred_element_type=F32)


def _dot_nt(a, b):
    return lax.dot_general(a, b, (((1,), (1,)), ((), ())), preferred_element_type=F32)


def _lane_lo():
    return lax.broadcasted_iota(jnp.int32, (1, LANES), 1) < HEAD_DIM


def _branch_gate(g, branch, c, lo):
    l0 = branch * B_HEADS + 2 * c
    return jnp.where(lo, g[:, l0:l0 + 1], g[:, l0 + 1:l0 + 2])


def _ffn_body(*refs, chunks, final):
    if final:
        x_ref, g_ref, w1_ref, w3_ref, w2_ref, gf_ref, o_ref, acc_ref = refs
    else:
        x_ref, g_ref, w1_ref, w3_ref, w2_ref, o_ref, acc_ref = refs
    xn = _rms(x_ref[...], g_ref[...]).astype(BF16)
    f0 = 0
    for n, fc in enumerate(chunks):
        a = _dot(xn, w1_ref[:, f0:f0 + fc])
        b = _dot(xn, w3_ref[:, f0:f0 + fc])
        part = _dot((a * jax.nn.sigmoid(a) * b).astype(BF16), w2_ref[f0:f0 + fc, :])
        if n == 0:
            acc_ref[...] = part
        else:
            acc_ref[...] += part
        f0 += fc
    y = x_ref[...] + 0.5 * acc_ref[...]
    if final:
        y = _rms(y, gf_ref[...])
    o_ref[...] = y


def _resident(shape):
    return pl.BlockSpec(shape, lambda i: (0,) * len(shape), pipeline_mode=pl.Buffered(1))


def _ffn(x, g, w1, w3, w2, g_final=None, *, tm=512, chunks=(768, 768, 768, 512)):
    n, d = x.shape
    assert sum(chunks) == D_FF
    final = g_final is not None
    in_specs = [pl.BlockSpec((tm, d), lambda i: (i, 0)), _resident((1, d)),
                _resident(w1.shape), _resident(w3.shape), _resident(w2.shape)]
    args = [x, g.reshape(1, d), w1, w3, w2]
    if final:
        in_specs.append(_resident((1, d)))
        args.append(g_final.reshape(1, d))
    return pl.pallas_call(
        functools.partial(_ffn_body, chunks=chunks, final=final),
        grid=(n // tm,),
        in_specs=in_specs,
        out_specs=pl.BlockSpec((tm, d), lambda i: (i, 0)),
        out_shape=jax.ShapeDtypeStruct((n, d), F32),
        scratch_shapes=[pltpu.VMEM((tm, d), F32)],
        compiler_params=_cparams(("parallel",)),
        name="ffn",
    )(*args)


def _col_chunks(width, step):
    return [(c0, min(step, width - c0)) for c0 in range(0, width, step)]


def _proj_body(x_ref, g_ref, pos_ref, inv_ref, sa_ref, sb_ref, wr_ref, wp_ref, ws_ref, r_ref, p_ref, s_ref, *, step):
    xn = _rms(x_ref[...], g_ref[...]).astype(BF16)
    ang = pos_ref[...] * inv_ref[...]
    cos = jnp.cos(ang)
    sin = jnp.sin(ang)
    sina = sin * sa_ref[...]
    sinb = sin * sb_ref[...]
    half = ROT_DIM // 2
    for c0, w in _col_chunks(R_WIDTH, step):
        acc = _dot(xn, wr_ref[:, c0:c0 + w])
        for c in range(w // LANES):
            y = acc[:, c * LANES:(c + 1) * LANES]
            up = pltpu.roll(y, LANES - half, axis=1)
            dn = pltpu.roll(y, half, axis=1)
            r_ref[:, c0 + c * LANES:c0 + (c + 1) * LANES] = (y * cos + up * sina + dn * sinb).astype(r_ref.dtype)
    for c0, w in _col_chunks(P_WIDTH, step):
        p_ref[:, c0:c0 + w] = _dot(xn, wp_ref[:, c0:c0 + w]).astype(p_ref.dtype)
    for c0, w in _col_chunks(G_WIDTH, step):
        s_ref[:, c0:c0 + w] = jax.nn.sigmoid(_dot(xn, ws_ref[:, c0:c0 + w])).astype(s_ref.dtype)


def _proj(h, g, w_rope, w_plain, w_sig, rope_in, *, tm=256, step=768):
    n, d = h.shape
    posf, inv, sa, sb = rope_in
    row = lambda w: pl.BlockSpec((tm, w), lambda i: (i, 0))
    return pl.pallas_call(
        functools.partial(_proj_body, step=step),
        grid=(n // tm,),
        in_specs=[row(d), _resident((1, d)), row(1)] + [_resident((1, LANES))] * 3
        + [_resident(w_rope.shape), _resident(w_plain.shape), _resident(w_sig.shape)],
        out_specs=[row(R_WIDTH), row(P_WIDTH), row(G_WIDTH)],
        out_shape=[jax.ShapeDtypeStruct((n, w), BF16) for w in (R_WIDTH, P_WIDTH, G_WIDTH)],
        compiler_params=_cparams(("parallel",)),
        name="proj",
    )(h, g.reshape(1, d), posf, inv, sa, sb, w_rope, w_plain, w_sig)


def _attn_a_body(q_ref, k_ref, v_ref, o_ref, qf, kf, vf, m_s, l_s, acc_s, *, seq):
    g = pl.program_id(2)
    qf[...] = q_ref[...].astype(F32)
    kf[...] = k_ref[...].astype(F32)
    vf[...] = v_ref[...].astype(F32)
    lo = _lane_lo()
    blk = LANES
    qi = lax.broadcasted_iota(jnp.int32, (blk, 1), 0) + blk
    ki = lax.broadcasted_iota(jnp.int32, (1, 2 * blk), 1)
    dist = qi - ki

    def run_group(window, dil, first):
        steps = window // dil
        per = seq // (blk * dil)
        in_band = (dist >= 0) & (dist <= steps)

        def rows(ref, start):
            if dil == 1:
                return ref[pl.ds(pl.multiple_of(start, blk), blk), :]
            return ref[pl.ds(start, blk, stride=dil), :]

        ub = min(4, per)

        def body(it, carry):
            r = (it * ub) // per
            j0 = it * ub - r * per
            base = r + dil * blk * j0
            prev = jnp.maximum(base - dil * blk, r)
            kbs = [rows(kf, prev).astype(BF16)] + [rows(kf, base + u * dil * blk).astype(BF16) for u in range(ub)]
            vbs = [rows(vf, prev).astype(BF16)] + [rows(vf, base + u * dil * blk).astype(BF16) for u in range(ub)]
            for u in range(ub):
                block(base + u * dil * blk, jnp.concatenate(kbs[u:u + 2], axis=0),
                      jnp.concatenate(vbs[u:u + 2], axis=0), in_band & ((ki >= blk) | (j0 + u > 0)))
            return carry

        def block(start, kcat, vcat, valid):
            qb = rows(qf, start)
            ms, ls, os_ = [], [], []
            for half in range(2):
                qm = jnp.where(lo if half == 0 else ~lo, qb, 0.0).astype(BF16)
                s = jnp.where(valid, _dot_nt(qm, kcat), NEG)
                m = jnp.max(s, axis=1, keepdims=True)
                p = jnp.exp2(s - m)
                ls.append(jnp.sum(p, axis=1, keepdims=True))
                ms.append(m)
                os_.append(_dot(p.astype(BF16), vcat))
            m_b = jnp.where(lo, ms[0], ms[1])
            l_b = jnp.where(lo, ls[0], ls[1])
            o_b = jnp.where(lo, os_[0], os_[1])
            sl = (pl.ds(pl.multiple_of(start, blk), blk) if dil == 1 else pl.ds(start, blk, stride=dil), slice(None))
            if first:
                m_s[sl] = m_b
                l_s[sl] = l_b
                acc_s[sl] = o_b
            else:
                m_old = m_s[sl]
                m_new = jnp.maximum(m_old, m_b)
                a_old = jnp.exp2(m_old - m_new)
                a_b = jnp.exp2(m_b - m_new)
                m_s[sl] = m_new
                l_s[sl] = l_s[sl] * a_old + l_b * a_b
                acc_s[sl] = acc_s[sl] * a_old + o_b * a_b

        lax.fori_loop(0, seq // (blk * ub), body, 0, unroll=8 // ub)

    for step, (window, dil) in enumerate(reversed(DIL_PAIRS)):
        pl.when(g == step)(functools.partial(run_group, window, dil, step == 0))

    @pl.when(g == A_GROUPS - 1)
    def _():
        o_ref[...] = (acc_s[...] / l_s[...]).astype(o_ref.dtype)


def _attn_a(r3, p3):
    bsz, seq, _ = r3.shape
    npair = A_SLOTS // 2
    blk = lambda off: pl.BlockSpec((None, seq, LANES), lambda b, p, g: (b, 0, off + (A_GROUPS - 1 - g) * npair + p))
    return pl.pallas_call(
        functools.partial(_attn_a_body, seq=seq),
        grid=(bsz, npair, A_GROUPS),
        in_specs=[blk(R_AQ), blk(R_AK), blk(P_AV)],
        out_specs=pl.BlockSpec((None, seq, LANES), lambda b, p, g: (b, 0, p)),
        out_shape=jax.ShapeDtypeStruct((bsz, seq, A_WIDTH), BF16),
        scratch_shapes=[pltpu.VMEM((seq, LANES), F32)] * 6,
        compiler_params=_cparams(("parallel", "parallel", "arbitrary")),
        name="attn_dilated",
    )(r3, r3, p3)


def _compress_body(x_ref, pe_ref, wlo_ref, whi_ref, w2_ref, o_ref, *, nc):
    x = x_ref[...].astype(F32)
    ylo = _dot((x + pe_ref[0:1, :]).astype(BF16), wlo_ref[...])
    yhi = _dot((x + pe_ref[1:2, :]).astype(BF16), whi_ref[...])
    pre = ylo + pltpu.roll(yhi, nc - 1, axis=0)
    o_ref[...] = _dot(jax.nn.gelu(pre).astype(BF16), w2_ref[...]).astype(o_ref.dtype)


def _compress(xc, pe, wlo, whi, w2):
    bsz, _, nc, width = xc.shape
    hid = wlo.shape[-1]
    return pl.pallas_call(
        functools.partial(_compress_body, nc=nc),
        grid=(bsz, 2),
        in_specs=[
            pl.BlockSpec((None, None, nc, width), lambda b, t: (b, t, 0, 0)),
            pl.BlockSpec((None, 2, width), lambda b, t: (t, 0, 0)),
            pl.BlockSpec((None, width, hid), lambda b, t: (t, 0, 0)),
            pl.BlockSpec((None, width, hid), lambda b, t: (t, 0, 0)),
            pl.BlockSpec((None, hid, LANES), lambda b, t: (t, 0, 0)),
        ],
        out_specs=pl.BlockSpec((None, None, nc, LANES), lambda b, t: (b, t, 0, 0)),
        out_shape=jax.ShapeDtypeStruct((bsz, 2, nc, LANES), BF16),
        compiler_params=_cparams(("parallel", "parallel")),
        name="nsa_compress",
    )(xc, pe, wlo, whi, w2)


def _split3(x):
    hi = x.astype(BF16)
    r1 = x - hi.astype(F32)
    mid = r1.astype(BF16)
    lo = (r1 - mid.astype(F32)).astype(BF16)
    return hi, mid, lo


def _cmp_body(q_ref, kcv_ref, mm_ref, gate_ref, ocmp_ref, sel_ref, *, tq, nc):
    t0 = pl.program_id(1) * tq
    q = q_ref[...]
    kc = kcv_ref[0]
    vc = kcv_ref[1]
    lo = _lane_lo()
    t = t0 + lax.broadcasted_iota(jnp.int32, (tq, 1), 0)
    cend = lax.broadcasted_iota(jnp.int32, (1, nc), 1) * CMP_STRIDE + (CMP_BLOCK - 1)
    valid = cend <= t
    gates = gate_ref[...].astype(F32)
    imp = [jnp.zeros((tq, nc), F32), jnp.zeros((tq, nc), F32)]
    for c in range(B_HEADS // 2):
        qc = q[:, c * LANES:(c + 1) * LANES]
        outs = []
        for half in range(2):
            qm = jnp.where(lo if half == 0 else ~lo, qc, jnp.zeros_like(qc))
            s = jnp.where(valid, _dot_nt(qm, kc), NEG)
            m = jnp.max(s, axis=1, keepdims=True)
            p = jnp.where(valid, jnp.exp2(s - m), 0.0)
            l = jnp.sum(p, axis=1, keepdims=True)
            pn = p * (1.0 / jnp.where(l > 0, l, 1.0))
            imp[half] = imp[half] + pn
            outs.append(_dot(pn.astype(BF16), vc))
        oc = jnp.where(lo, outs[0], outs[1])
        ocmp_ref[:, c * LANES:(c + 1) * LANES] = oc * _branch_gate(gates, 0, c, lo)

    score = jnp.zeros((tq, LANES), F32)
    for half in range(2):
        for part in _split3(imp[half]):
            score = score + _dot(part, mm_ref[half])
    st = score.T
    row = lax.broadcasted_iota(jnp.int32, (LANES, 1), 0)
    jb = jnp.where(row < SEL_BLOCK, row, row - SEL_BLOCK)
    tl = t0 + lax.broadcasted_iota(jnp.int32, (1, tq), 1)
    cur = tl // SEL_BLOCK
    forced = (jb == 0) | (jb == cur) | (jb == cur - 1)
    st = jnp.where(forced, FORCE_SCORE, jnp.where(jb * SEL_BLOCK <= tl, st, -1.0))
    sub = 8
    sub_row = lax.broadcasted_iota(jnp.int32, (sub, 1), 0)
    picked = []
    for half in range(2):
        chunks = [st[half * SEL_BLOCK + k * sub:half * SEL_BLOCK + (k + 1) * sub] for k in range(SEL_BLOCK // sub)]
        ranks = [jnp.zeros((sub, tq), F32) for _ in chunks]
        for jp in range(SEL_BLOCK):
            other = chunks[jp // sub][jp % sub:jp % sub + 1]
            for k, ch in enumerate(chunks):
                if k < jp // sub:
                    beats = other > ch
                elif k > jp // sub:
                    beats = other >= ch
                else:
                    beats = (other > ch) | ((other == ch) & (sub_row > jp % sub))
                ranks[k] = ranks[k] + jnp.where(beats, 1.0, 0.0)
        picked += [jnp.where(r < N_SELECT, 1.0, 0.0) for r in ranks]
    sel_ref[...] = jnp.concatenate(picked, axis=0)


def _cmp_select(p3, kcv, mm, g3, *, tq=256):
    bsz, seq, _ = p3.shape
    nc = kcv.shape[2]
    return pl.pallas_call(
        functools.partial(_cmp_body, tq=tq, nc=nc),
        grid=(bsz, seq // tq),
        in_specs=[
            pl.BlockSpec((None, tq, B_WIDTH), lambda b, i: (b, i, 0)),
            pl.BlockSpec((None, 2, nc, LANES), lambda b, i: (b, 0, 0, 0)),
            pl.BlockSpec((2, nc, LANES), lambda b, i: (0, 0, 0)),
            pl.BlockSpec((None, tq, LANES), lambda b, i: (b, i, G_BRANCH)),
        ],
        out_specs=[
            pl.BlockSpec((None, tq, B_WIDTH), lambda b, i: (b, i, 0)),
            pl.BlockSpec((None, 2 * SEL_BLOCK, tq), lambda b, i: (b, 0, i)),
        ],
        out_shape=[
            jax.ShapeDtypeStruct((bsz, seq, B_WIDTH), F32),
            jax.ShapeDtypeStruct((bsz, 2 * SEL_BLOCK, seq), F32),
        ],
        compiler_params=_cparams(("parallel", "parallel")),
        name="nsa_cmp_select",
    )(p3, kcv, mm, g3)


def _selwin_body(q_ref, ks_ref, vs_ref, kw_ref, vw_ref, sel_ref, ocmp_ref, gate_ref, o_ref,
                 vst_ref, vwt_ref, hot_ref, qs_ref, m_ref, acc_ref, *, tq, tk, seq):
    i = pl.program_id(1)
    t0 = i * tq
    nh = B_HEADS // 2
    nblk = tk // SEL_BLOCK
    lo = _lane_lo()
    lane = lax.broadcasted_iota(jnp.int32, (1, LANES), 1)

    @pl.when(i == 0)
    def _():
        step = 512
        for r0 in range(0, seq, step):
            vs_t = vs_ref[r0:r0 + step, :].astype(F32).T.astype(BF16)
            vw_t = vw_ref[r0:r0 + step, :].astype(F32).T.astype(BF16)
            for half in range(2):
                vst_ref[half, 0:HEAD_DIM, r0:r0 + step] = vs_t[half * HEAD_DIM:(half + 1) * HEAD_DIM]
                vwt_ref[half, 0:HEAD_DIM, r0:r0 + step] = vw_t[half * HEAD_DIM:(half + 1) * HEAD_DIM]
        ones = jnp.ones((2, ONES_ROWS, seq), BF16)
        vst_ref[:, HEAD_DIM:, :] = ones
        vwt_ref[:, HEAD_DIM:, :] = ones
        hot_ref[...] = jnp.where(lax.broadcasted_iota(jnp.int32, (tk, LANES), 0) // SEL_BLOCK
                                 == lax.broadcasted_iota(jnp.int32, (tk, LANES), 1), 1.0, 0.0).astype(BF16)

    qall = q_ref[...]
    for half in range(2):
        hm = lo if half == 0 else ~lo
        qs_ref[half] = jnp.concatenate(
            [jnp.where(hm, qall[:, c * LANES:(c + 1) * LANES], jnp.zeros((tq, LANES), BF16)) for c in range(nh)], axis=0)

    tq_lane = t0 + lax.broadcasted_iota(jnp.int32, (1, tq), 1)

    def softmax_step(st, vt, m_old):
        m_new = jnp.max(st, axis=0, keepdims=True)
        if m_old is not None:
            m_new = jnp.maximum(m_old, m_new)
        return m_new, _dot(vt, jnp.exp2(st - m_new).astype(BF16))

    span = WINDOW + tq
    wstart = pl.multiple_of(jnp.maximum(t0 - WINDOW, 0), LANES)
    dw = tq_lane - (wstart + lax.broadcasted_iota(jnp.int32, (span, 1), 0))
    wbias = jnp.where((dw >= 0) & (dw < WINDOW), 0.0, NEG)
    wbias = jnp.concatenate([wbias] * nh, axis=1)
    kw = kw_ref[pl.ds(wstart, span), :]
    o_win = []
    for half in range(2):
        _, pv = softmax_step(_dot_nt(kw, qs_ref[half]) + wbias, vwt_ref[half, :, pl.ds(wstart, span)], None)
        o_win.append(pv[:HEAD_DIM] * (1.0 / pv[HEAD_DIM:HEAD_DIM + 1]))

    m_ref[...] = jnp.full(m_ref.shape, NEG, F32)
    acc_ref[...] = jnp.zeros(acc_ref.shape, F32)
    unpicked = jnp.where(sel_ref[...].T > 0.5, 0.0, NEG)

    def tile(kt, diagonal):
        koff = pl.multiple_of(kt * tk, tk)
        ks = jnp.concatenate([ks_ref[pl.ds(koff, tk), :], hot_ref[...]], axis=1)
        for half in range(2):
            code = pltpu.roll(unpicked, (2 * LANES - half * SEL_BLOCK - kt * nblk) % LANES, axis=1)
            code = jnp.where(lane < nblk, code, 0.0).astype(BF16)
            st = _dot_nt(ks, jnp.concatenate([qs_ref[half], jnp.concatenate([code] * nh, axis=0)], axis=1))
            if diagonal:
                causal = (koff + lax.broadcasted_iota(jnp.int32, (tk, 1), 0)) <= tq_lane
                st = st + jnp.concatenate([jnp.where(causal, 0.0, NEG)] * nh, axis=1)
            m_old = m_ref[half]
            m_new, pv = softmax_step(st, vst_ref[half, :, pl.ds(koff, tk)], m_old)
            m_ref[half] = m_new
            acc_ref[half] = jnp.exp2(m_old - m_new) * acc_ref[half] + pv

    last = (t0 + tq + tk - 1) // tk - 1

    def kt_body(kt, carry):
        tile(kt, False)
        return carry

    lax.fori_loop(0, last, kt_body, 0)
    tile(last, True)

    gates = gate_ref[...].astype(F32)
    o_sel = [acc_ref[half, 0:HEAD_DIM, :] * (1.0 / acc_ref[half, HEAD_DIM:HEAD_DIM + 1, :]) for half in range(2)]
    for c in range(nh):
        cs = slice(c * LANES, (c + 1) * LANES)
        ls = slice(c * tq, (c + 1) * tq)
        osel = jnp.concatenate([o_sel[0][:, ls], o_sel[1][:, ls]], axis=0).T
        owin = jnp.concatenate([o_win[0][:, ls], o_win[1][:, ls]], axis=0).T
        o_ref[:, cs] = (ocmp_ref[:, cs] + _branch_gate(gates, 1, c, lo) * osel
                        + _branch_gate(gates, 2, c, lo) * owin).astype(o_ref.dtype)


def _sel_win(r3, p3, sel, ocmp, g3, *, tq=256, tk=1024):
    bsz, seq, _ = r3.shape
    rows = (B_HEADS // 2) * tq
    full = lambda off: pl.BlockSpec((None, seq, LANES), lambda b, i: (b, 0, off))
    tile = lambda off: pl.BlockSpec((None, tq, B_WIDTH), lambda b, i: (b, i, off))
    return pl.pallas_call(
        functools.partial(_selwin_body, tq=tq, tk=tk, seq=seq),
        grid=(bsz, seq // tq),
        in_specs=[tile(0), full(R_KSLC), full(P_VSLC), full(R_KWIN), full(P_VWIN),
                  pl.BlockSpec((None, 2 * SEL_BLOCK, tq), lambda b, i: (b, 0, i)), tile(0),
                  pl.BlockSpec((None, tq, LANES), lambda b, i: (b, i, G_BRANCH))],
        out_specs=tile(0),
        out_shape=jax.ShapeDtypeStruct((bsz, seq, B_WIDTH), BF16),
        scratch_shapes=[pltpu.VMEM((2, HEAD_DIM + ONES_ROWS, seq), BF16)] * 2
        + [pltpu.VMEM((tk, LANES), BF16), pltpu.VMEM((2, rows, LANES), BF16),
           pltpu.VMEM((2, 1, rows), F32), pltpu.VMEM((2, HEAD_DIM + ONES_ROWS, rows), F32)],
        compiler_params=_cparams(("parallel", "arbitrary")),
        name="nsa_sel_win",
    )(r3, r3, p3, r3, p3, sel, ocmp, g3)


def _stick_body(q_ref, k_ref, v_ref, o_ref, carry_ref, acc_ref, *, tq, tk):
    i = pl.program_id(2)
    t0 = i * tq
    q = q_ref[...]
    lo = _lane_lo()
    qh = [jnp.where(lo, q, jnp.zeros_like(q)), jnp.where(lo, jnp.zeros_like(q), q)]
    tpos = t0 + lax.broadcasted_iota(jnp.int32, (tq, 1), 0)
    later = jnp.where(lax.broadcasted_iota(jnp.int32, (tk, tk), 0) > lax.broadcasted_iota(jnp.int32, (tk, tk), 1),
                      1.0, 0.0).astype(BF16)
    carry_ref[...] = jnp.zeros_like(carry_ref)
    acc_ref[...] = jnp.zeros_like(acc_ref)
    ndiag = tq // tk

    def tile(koff, r0, r1, masked):
        k = k_ref[pl.ds(koff, tk), :]
        v = v_ref[pl.ds(koff, tk), :]
        rs = slice(r0, r1)
        if masked:
            before = (koff + lax.broadcasted_iota(jnp.int32, (1, tk), 1)) < tpos[rs]
        for half in range(2):
            z = _dot_nt(qh[half][rs], k)
            sp = jnp.maximum(z, 0.0) + jnp.log(1.0 + jnp.exp(-jnp.abs(z)))
            spm = jnp.where(before, sp, 0.0) if masked else sp
            inside = _dot(spm.astype(BF16), later)
            c = carry_ref[half, rs, :]
            a = jnp.exp((z - sp) + (jnp.concatenate([c] * (tk // LANES), axis=1) - inside))
            if masked:
                a = jnp.where(before, a, 0.0)
            acc_ref[half, rs, :] += _dot(a.astype(BF16), v)
            carry_ref[half, rs, :] = c - jnp.sum(spm, axis=1, keepdims=True)

    for d in reversed(range(ndiag)):
        koff = pl.multiple_of(t0 + d * tk, tk)
        tile(koff, d * tk, (d + 1) * tk, True)
        if (d + 1) * tk < tq:
            tile(koff, (d + 1) * tk, tq, False)

    def body(n, c):
        tile(pl.multiple_of(t0 - (n + 1) * tk, tk), 0, tq, False)
        return c

    lax.fori_loop(0, i * ndiag, body, 0)
    o_ref[...] = jnp.where(lo, acc_ref[0], acc_ref[1]).astype(o_ref.dtype)


def _stick(p3, *, tq=1024, tk=256):
    bsz, seq, _ = p3.shape
    npair = C_HEADS // 2
    full = lambda off: pl.BlockSpec((None, seq, LANES), lambda b, p, i: (b, 0, off + p))
    return pl.pallas_call(
        functools.partial(_stick_body, tq=tq, tk=tk),
        grid=(bsz, npair, seq // tq),
        in_specs=[pl.BlockSpec((None, tq, LANES), lambda b, p, i: (b, i, P_CQ + p)), full(P_CK), full(P_CV)],
        out_specs=pl.BlockSpec((None, tq, LANES), lambda b, p, i: (b, i, p)),
        out_shape=jax.ShapeDtypeStruct((bsz, seq, C_WIDTH), BF16),
        scratch_shapes=[pltpu.VMEM((2, tq, LANES), F32)] * 2,
        compiler_params=_cparams(("parallel", "parallel", "parallel")),
        name="stick_breaking",
    )(p3, p3, p3)


def _merge_body(h_ref, oa_ref, ob_ref, oc_ref, ga_ref, gb_ref, gc_ref, wa_ref, wb_ref, wc_ref, wo_ref, o_ref):
    y = (ga_ref[...].astype(F32) * _dot(oa_ref[...], wa_ref[...])
         + gb_ref[...].astype(F32) * _dot(ob_ref[...], wb_ref[...])
         + gc_ref[...].astype(F32) * _dot(oc_ref[...], wc_ref[...]))
    o_ref[...] = h_ref[...] + _dot(y.astype(BF16), wo_ref[...])


def _merge(h, oa, ob, oc, gates, wa, wb, wc, wo, *, tm=512):
    n, d = h.shape
    row = lambda w: pl.BlockSpec((tm, w), lambda i: (i, 0))
    whole = lambda a: pl.BlockSpec(a.shape, lambda i: (0, 0))
    gate = lambda m: pl.BlockSpec((tm, d), lambda i: (i, m))
    return pl.pallas_call(
        _merge_body,
        grid=(n // tm,),
        in_specs=[row(d), row(A_WIDTH), row(B_WIDTH), row(C_WIDTH), gate(0), gate(1), gate(2),
                  whole(wa), whole(wb), whole(wc), whole(wo)],
        out_specs=row(d),
        out_shape=jax.ShapeDtypeStruct((n, d), F32),
        compiler_params=_cparams(("parallel",)),
        name="merge_out",
    )(h, oa, ob, oc, gates, gates, gates, wa, wb, wc, wo)


def _prep_layer(w_in, w_gate, w_up, pe_k, cw1_k, cw2_k, pe_v, cw1_v, cw2_v):
    b0 = A_IN
    kv0 = b0 + B_WIDTH
    gt0 = kv0 + B_KV_IN
    c0 = b0 + B_IN
    d = w_in.shape[0]
    nh = B_HEADS // 2
    kvcol = lambda br, kvt: kv0 + (br * 2 + kvt) * B_KV * HEAD_DIM
    span = lambda s, w: w_in[:, s:s + w]
    wq_b = (span(b0, B_WIDTH).reshape(d, 2, nh, HEAD_DIM).transpose(0, 2, 1, 3).reshape(d, B_WIDTH) * Q_SCALE2).astype(BF16)
    aw = A_GROUPS * A_WIDTH
    w_rope = jnp.concatenate([
        wq_b, (span(0, aw) * Q_SCALE2).astype(BF16), span(aw, aw).astype(BF16),
        span(kvcol(1, 0), LANES).astype(BF16), span(kvcol(2, 0), LANES).astype(BF16)], axis=1)
    w_plain = jnp.concatenate([
        wq_b, span(kvcol(0, 0), 2 * LANES).astype(BF16),
        span(kvcol(1, 1), LANES).astype(BF16), span(kvcol(2, 1), LANES).astype(BF16),
        span(2 * aw, aw).astype(BF16), (span(c0, C_WIDTH) * Q_SCALE).astype(BF16),
        span(c0 + C_WIDTH, 2 * C_WIDTH).astype(BF16)], axis=1)
    wg = span(gt0, 3 * B_HEADS).reshape(d, 2, nh, 3).transpose(0, 3, 2, 1).reshape(d, 3 * B_HEADS)
    w_sig = jnp.concatenate([w_gate.astype(BF16), wg.astype(BF16),
                             jnp.zeros((d, G_WIDTH - 3 * D_MODEL - 3 * B_HEADS), BF16)], axis=1)

    def cmp_weights(pe, w1, w2):
        w1r = w1.reshape(2, CMP_STRIDE, HEAD_DIM, CMP_HIDDEN)
        z = jnp.zeros_like(w1r)
        per_head = [jnp.stack([w1r, z], axis=2), jnp.stack([z, w1r], axis=2)]
        wfull = jnp.concatenate(per_head, axis=-1).reshape(2, CMP_STRIDE * LANES, 2 * CMP_HIDDEN)
        zz = jnp.zeros_like(w2)
        w2f = jnp.concatenate([jnp.concatenate([w2, zz], axis=1), jnp.concatenate([zz, w2], axis=1)], axis=0)
        per = pe.reshape(2, CMP_STRIDE, 1, HEAD_DIM)
        pef = jnp.broadcast_to(per, (2, CMP_STRIDE, B_KV, HEAD_DIM)).reshape(2, CMP_STRIDE * LANES)
        return pef, wfull[0].astype(BF16), wfull[1].astype(BF16), w2f.astype(BF16)

    ck, cv = cmp_weights(pe_k, cw1_k, cw2_k), cmp_weights(pe_v, cw1_v, cw2_v)
    cmp_w = tuple(jnp.stack([a, b]) for a, b in zip(ck, cv))
    w_up = w_up.astype(BF16)
    wb = w_up[A_WIDTH:A_WIDTH + B_WIDTH].reshape(2, nh, HEAD_DIM, -1).transpose(1, 0, 2, 3).reshape(B_WIDTH, -1)
    return dict(w_rope=w_rope, w_plain=w_plain, w_sig=w_sig, cmp_w=cmp_w,
                wa=w_up[:A_WIDTH], wb=wb, wc=w_up[A_WIDTH + B_WIDTH:])


def _rope_consts():
    inv = ROPE_THETA ** (-jnp.arange(0, ROT_DIM, 2, dtype=F32) / ROT_DIM)
    lane = np.arange(LANES) % HEAD_DIM
    half = ROT_DIM // 2
    inv_l = jnp.where(lane < ROT_DIM, inv[lane % half], 0.0).reshape(1, LANES).astype(F32)
    sa = np.where(lane < half, -1.0, 0.0).reshape(1, LANES).astype(np.float32)
    sb = np.where((lane >= half) & (lane < ROT_DIM), 1.0, 0.0).reshape(1, LANES).astype(np.float32)
    return inv_l, jnp.asarray(sa), jnp.asarray(sb)


def _block_sum_matrix(nc):
    ratio = SEL_BLOCK // CMP_STRIDE
    per = CMP_BLOCK // CMP_STRIDE
    m = np.zeros((2, nc, LANES), np.float32)
    for j in range(SEL_BLOCK):
        for a in range(ratio):
            for b in range(per):
                c = ratio * j + a + b
                if c < nc - 1:
                    m[0, c, j] += 1.0
                    m[1, c, SEL_BLOCK + j] += 1.0
    return jnp.asarray(m).astype(BF16)


def _mixers(h, lw, norm_mix, rope_in, bsz, seq):
    n = bsz * seq
    r, p, g = _proj(h, norm_mix, lw["w_rope"], lw["w_plain"], lw["w_sig"], rope_in)
    r3, p3, g3 = r.reshape(bsz, seq, R_WIDTH), p.reshape(bsz, seq, P_WIDTH), g.reshape(bsz, seq, G_WIDTH)
    oa = _attn_a(r3, p3)
    nc = seq // CMP_STRIDE
    xc = p3[:, :, P_KCMP * LANES:(P_KCMP + 2) * LANES].reshape(bsz, nc, CMP_STRIDE, 2, LANES)
    xc = xc.transpose(0, 3, 1, 2, 4).reshape(bsz, 2, nc, CMP_STRIDE * LANES)
    kcv = _compress(xc, *lw["cmp_w"])
    ocmp, sel = _cmp_select(p3, kcv, _block_sum_matrix(nc), g3)
    ob = _sel_win(r3, p3, sel, ocmp, g3)
    oc = _stick(p3)
    return oa.reshape(n, A_WIDTH), ob.reshape(n, B_WIDTH), oc.reshape(n, C_WIDTH), g


def kernel(x, positions, norm_ffn1, ffn1_w1, ffn1_w3, ffn1_w2, norm_mix, w_in, cmp_pe_k, cmp_w1_k, cmp_w2_k,
           cmp_pe_v, cmp_w1_v, cmp_w2_v, w_gate, w_up, w_out, norm_ffn2, ffn2_w1, ffn2_w3, ffn2_w2, norm_final):
    bsz, seq, d = x.shape
    n = bsz * seq
    depth = w_in.shape[0]
    rope_in = (positions.astype(F32).reshape(n, 1),) + _rope_consts()
    h = x.reshape(n, d)
    for i in range(depth):
        lw = _prep_layer(w_in[i], w_gate[i], w_up[i], cmp_pe_k[i], cmp_w1_k[i], cmp_w2_k[i],
                         cmp_pe_v[i], cmp_w1_v[i], cmp_w2_v[i])
        h = _ffn(h, norm_ffn1[i], ffn1_w1[i].astype(BF16), ffn1_w3[i].astype(BF16), ffn1_w2[i].astype(BF16))
        oa, ob, oc, g = _mixers(h, lw, norm_mix[i], rope_in, bsz, seq)
        h = _merge(h, oa, ob, oc, g, lw["wa"], lw["wb"], lw["wc"], w_out[i].astype(BF16))
        h = _ffn(h, norm_ffn2[i], ffn2_w1[i].astype(BF16), ffn2_w3[i].astype(BF16), ffn2_w2[i].astype(BF16),
                 norm_final if i == depth - 1 else None)
    return h.reshape(bsz, seq, d)
```

```python
import functools

import numpy as np
import jax
import jax.numpy as jnp
from jax import lax
from jax.experimental import pallas as pl
from jax.experimental.pallas import tpu as pltpu

F32 = jnp.float32
BF16 = jnp.bfloat16

D_MODEL = 1024
HEAD_DIM = 64
ROT_DIM = HEAD_DIM // 4
ROPE_THETA = 500000.0
NORM_EPS = 1e-6
D_FF = 2816
LANES = 128

DIL_PAIRS = ((128, 1), (512, 4), (2048, 16))
A_GROUPS = 3
A_SLOTS = 6
A_WIDTH = A_SLOTS * HEAD_DIM
B_HEADS = 8
B_KV = 2
B_WIDTH = B_HEADS * HEAD_DIM
CMP_BLOCK = 32
CMP_STRIDE = 16
CMP_HIDDEN = 2 * HEAD_DIM
SEL_BLOCK = 64
N_SELECT = 16
WINDOW = 512
FORCE_SCORE = 1e4
C_HEADS = 6
C_WIDTH = C_HEADS * HEAD_DIM
A_IN = 3 * A_GROUPS * A_WIDTH
B_KV_IN = 3 * 2 * B_KV * HEAD_DIM
B_IN = B_WIDTH + B_KV_IN + 3 * B_HEADS
C_IN = 3 * C_WIDTH
Q_SCALE = HEAD_DIM ** -0.5
LOG2E = float(np.log2(np.e))
Q_SCALE2 = Q_SCALE * LOG2E

NEG = -1e30
ONES_ROWS = 16
VMEM_LIMIT = 48 * 1024 * 1024

R_WIDTH = 3072
R_AQ, R_AK, R_KSLC, R_KWIN = 4, 13, 22, 23
P_WIDTH = 3328
P_KCMP, P_VSLC, P_VWIN, P_AV, P_CQ, P_CK, P_CV = 4, 6, 7, 8, 17, 20, 23
G_WIDTH = 3328
G_BRANCH = 3 * D_MODEL // LANES


def _cparams(sem):
    return pltpu.CompilerParams(dimension_semantics=sem, vmem_limit_bytes=VMEM_LIMIT)


def _rms(x, g):
    return x * lax.rsqrt(jnp.mean(x * x, axis=-1, keepdims=True) + NORM_EPS) * g


def _dot(a, b):
    return jnp.dot(a, b, preferred_element_type=F32)


def _dot_nt(a, b):
    return lax.dot_general(a, b, (((1,), (1,)), ((), ())), preferred_element_type=F32)


def _lane_lo():
    return lax.broadcasted_iota(jnp.int32, (1, LANES), 1) < HEAD_DIM


def _branch_gate(g, branch, c, lo):
    l0 = branch * B_HEADS + 2 * c
    return jnp.where(lo, g[:, l0:l0 + 1], g[:, l0 + 1:l0 + 2])


def _ffn_body(*refs, chunks, final):
    if final:
        x_ref, g_ref, w1_ref, w3_ref, w2_ref, gf_ref, o_ref, acc_ref = refs
    else:
        x_ref, g_ref, w1_ref, w3_ref, w2_ref, o_ref, acc_ref = refs
    xn = _rms(x_ref[...], g_ref[...]).astype(BF16)
    f0 = 0
    for n, fc in enumerate(chunks):
        a = _dot(xn, w1_ref[:, f0:f0 + fc])
        b = _dot(xn, w3_ref[:, f0:f0 + fc])
        part = _dot((a * jax.nn.sigmoid(a) * b).astype(BF16), w2_ref[f0:f0 + fc, :])
        if n == 0:
            acc_ref[...] = part
        else:
            acc_ref[...] += part
        f0 += fc
    y = x_ref[...] + 0.5 * acc_ref[...]
    if final:
        y = _rms(y, gf_ref[...])
    o_ref[...] = y


def _resident(shape):
    return pl.BlockSpec(shape, lambda i: (0,) * len(shape), pipeline_mode=pl.Buffered(1))


def _ffn(x, g, w1, w3, w2, g_final=None, *, tm=512, chunks=(768, 768, 768, 512)):
    n, d = x.shape
    assert sum(chunks) == D_FF
    final = g_final is not None
    in_specs = [pl.BlockSpec((tm, d), lambda i: (i, 0)), _resident((1, d)),
                _resident(w1.shape), _resident(w3.shape), _resident(w2.shape)]
    args = [x, g.reshape(1, d), w1, w3, w2]
    if final:
        in_specs.append(_resident((1, d)))
        args.append(g_final.reshape(1, d))
    return pl.pallas_call(
        functools.partial(_ffn_body, chunks=chunks, final=final),
        grid=(n // tm,),
        in_specs=in_specs,
        out_specs=pl.BlockSpec((tm, d), lambda i: (i, 0)),
        out_shape=jax.ShapeDtypeStruct((n, d), F32),
        scratch_shapes=[pltpu.VMEM((tm, d), F32)],
        compiler_params=_cparams(("parallel",)),
        name="ffn",
    )(*args)


def _col_chunks(width, step):
    return [(c0, min(step, width - c0)) for c0 in range(0, width, step)]


def _proj_body(x_ref, g_ref, pos_ref, inv_ref, sa_ref, sb_ref, wr_ref, wp_ref, ws_ref, r_ref, p_ref, s_ref, xc_ref,
               kv_ref, *, step):
    xn = _rms(x_ref[...], g_ref[...]).astype(BF16)
    ang = pos_ref[...] * inv_ref[...]
    cos = jnp.cos(ang)
    sin = jnp.sin(ang)
    sina = sin * sa_ref[...]
    sinb = sin * sb_ref[...]
    half = ROT_DIM // 2
    for c0, w in _col_chunks(R_WIDTH, step):
        acc = _dot(xn, wr_ref[:, c0:c0 + w])
        for c in range(w // LANES):
            y = acc[:, c * LANES:(c + 1) * LANES]
            up = pltpu.roll(y, LANES - half, axis=1)
            dn = pltpu.roll(y, half, axis=1)
            r_ref[:, c0 + c * LANES:c0 + (c + 1) * LANES] = (y * cos + up * sina + dn * sinb).astype(r_ref.dtype)
    cmp0 = P_KCMP * LANES
    for c0, w in _col_chunks(P_WIDTH, step):
        acc = _dot(xn, wp_ref[:, c0:c0 + w])
        p_ref[:, c0:c0 + w] = acc.astype(p_ref.dtype)
        if c0 <= cmp0 and cmp0 + 2 * LANES <= c0 + w:
            rows = x_ref.shape[0] // CMP_STRIDE
            for t in range(2):
                kv_ref[t] = acc[:, cmp0 - c0 + t * LANES:cmp0 - c0 + (t + 1) * LANES]
                for pos in range(CMP_STRIDE):
                    xc_ref[t, :, pos * LANES:(pos + 1) * LANES] = (
                        kv_ref[t, pl.ds(pos, rows, stride=CMP_STRIDE), :].astype(xc_ref.dtype))
    for c0, w in _col_chunks(G_WIDTH, step):
        s_ref[:, c0:c0 + w] = jax.nn.sigmoid(_dot(xn, ws_ref[:, c0:c0 + w])).astype(s_ref.dtype)


def _proj(h, g, w_rope, w_plain, w_sig, rope_in, *, tm=256, step=768):
    n, d = h.shape
    posf, inv, sa, sb = rope_in
    row = lambda w: pl.BlockSpec((tm, w), lambda i: (i, 0))
    return pl.pallas_call(
        functools.partial(_proj_body, step=step),
        grid=(n // tm,),
        in_specs=[row(d), _resident((1, d)), row(1)] + [_resident((1, LANES))] * 3
        + [_resident(w_rope.shape), _resident(w_plain.shape), _resident(w_sig.shape)],
        out_specs=[row(R_WIDTH), row(P_WIDTH), row(G_WIDTH),
                   pl.BlockSpec((2, tm // CMP_STRIDE, CMP_STRIDE * LANES), lambda i: (0, i, 0))],
        out_shape=[jax.ShapeDtypeStruct((n, w), BF16) for w in (R_WIDTH, P_WIDTH, G_WIDTH)]
        + [jax.ShapeDtypeStruct((2, n // CMP_STRIDE, CMP_STRIDE * LANES), BF16)],
        scratch_shapes=[pltpu.VMEM((2, tm, LANES), F32)],
        compiler_params=_cparams(("parallel",)),
        name="proj",
    )(h, g.reshape(1, d), posf, inv, sa, sb, w_rope, w_plain, w_sig)


def _attn_a_body(q_ref, k_ref, v_ref, o_ref, qf, kf, vf, m_s, l_s, acc_s, *, seq):
    g = pl.program_id(2)
    qf[...] = q_ref[...].astype(F32)
    kf[...] = k_ref[...].astype(F32)
    vf[...] = v_ref[...].astype(F32)
    lo = _lane_lo()
    blk = LANES
    qi = lax.broadcasted_iota(jnp.int32, (blk, 1), 0) + blk
    ki = lax.broadcasted_iota(jnp.int32, (1, 2 * blk), 1)
    dist = qi - ki

    def run_group(window, dil, first):
        steps = window // dil
        per = seq // (blk * dil)
        in_band = (dist >= 0) & (dist <= steps)

        def rows(ref, start):
            if dil == 1:
                return ref[pl.ds(pl.multiple_of(start, blk), blk), :]
            return ref[pl.ds(start, blk, stride=dil), :]

        ub = min(4, per)

        def body(it, carry):
            r = (it * ub) // per
            j0 = it * ub - r * per
            base = r + dil * blk * j0
            prev = jnp.maximum(base - dil * blk, r)
            kbs = [rows(kf, prev).astype(BF16)] + [rows(kf, base + u * dil * blk).astype(BF16) for u in range(ub)]
            vbs = [rows(vf, prev).astype(BF16)] + [rows(vf, base + u * dil * blk).astype(BF16) for u in range(ub)]
            for u in range(ub):
                block(base + u * dil * blk, jnp.concatenate(kbs[u:u + 2], axis=0),
                      jnp.concatenate(vbs[u:u + 2], axis=0), in_band & ((ki >= blk) | (j0 + u > 0)))
            return carry

        def block(start, kcat, vcat, valid):
            qb = rows(qf, start)
            ms, ls, os_ = [], [], []
            for half in range(2):
                qm = jnp.where(lo if half == 0 else ~lo, qb, 0.0).astype(BF16)
                s = jnp.where(valid, _dot_nt(qm, kcat), NEG)
                m = jnp.max(s, axis=1, keepdims=True)
                p = jnp.exp2(s - m)
                ls.append(jnp.sum(p, axis=1, keepdims=True))
                ms.append(m)
                os_.append(_dot(p.astype(BF16), vcat))
            m_b = jnp.where(lo, ms[0], ms[1])
            l_b = jnp.where(lo, ls[0], ls[1])
            o_b = jnp.where(lo, os_[0], os_[1])
            sl = (pl.ds(pl.multiple_of(start, blk), blk) if dil == 1 else pl.ds(start, blk, stride=dil), slice(None))
            if first:
                m_s[sl] = m_b
                l_s[sl] = l_b
                acc_s[sl] = o_b
            else:
                m_old = m_s[sl]
                m_new = jnp.maximum(m_old, m_b)
                a_old = jnp.exp2(m_old - m_new)
                a_b = jnp.exp2(m_b - m_new)
                m_s[sl] = m_new
                l_s[sl] = l_s[sl] * a_old + l_b * a_b
                acc_s[sl] = acc_s[sl] * a_old + o_b * a_b

        lax.fori_loop(0, seq // (blk * ub), body, 0, unroll=8 // ub)

    for step, (window, dil) in enumerate(reversed(DIL_PAIRS)):
        pl.when(g == step)(functools.partial(run_group, window, dil, step == 0))

    @pl.when(g == A_GROUPS - 1)
    def _():
        o_ref[...] = (acc_s[...] / l_s[...]).astype(o_ref.dtype)


def _attn_a(r3, p3):
    bsz, seq, _ = r3.shape
    npair = A_SLOTS // 2
    blk = lambda off: pl.BlockSpec((None, seq, LANES), lambda b, p, g: (b, 0, off + (A_GROUPS - 1 - g) * npair + p))
    return pl.pallas_call(
        functools.partial(_attn_a_body, seq=seq),
        grid=(bsz, npair, A_GROUPS),
        in_specs=[blk(R_AQ), blk(R_AK), blk(P_AV)],
        out_specs=pl.BlockSpec((None, seq, LANES), lambda b, p, g: (b, 0, p)),
        out_shape=jax.ShapeDtypeStruct((bsz, seq, A_WIDTH), BF16),
        scratch_shapes=[pltpu.VMEM((seq, LANES), F32)] * 6,
        compiler_params=_cparams(("parallel", "parallel", "arbitrary")),
        name="attn_dilated",
    )(r3, r3, p3)


def _compress_body(x_ref, pe_ref, wlo_ref, whi_ref, w2_ref, o_ref, *, nc):
    x = x_ref[...].astype(F32)
    ylo = _dot((x + pe_ref[0:1, :]).astype(BF16), wlo_ref[...])
    yhi = _dot((x + pe_ref[1:2, :]).astype(BF16), whi_ref[...])
    pre = ylo + pltpu.roll(yhi, nc - 1, axis=0)
    o_ref[...] = _dot(jax.nn.gelu(pre).astype(BF16), w2_ref[...]).astype(o_ref.dtype)


def _compress(xc, pe, wlo, whi, w2):
    _, bsz, nc, width = xc.shape
    hid = wlo.shape[-1]
    return pl.pallas_call(
        functools.partial(_compress_body, nc=nc),
        grid=(bsz, 2),
        in_specs=[
            pl.BlockSpec((None, None, nc, width), lambda b, t: (t, b, 0, 0)),
            pl.BlockSpec((None, 2, width), lambda b, t: (t, 0, 0)),
            pl.BlockSpec((None, width, hid), lambda b, t: (t, 0, 0)),
            pl.BlockSpec((None, width, hid), lambda b, t: (t, 0, 0)),
            pl.BlockSpec((None, hid, LANES), lambda b, t: (t, 0, 0)),
        ],
        out_specs=pl.BlockSpec((None, None, nc, LANES), lambda b, t: (b, t, 0, 0)),
        out_shape=jax.ShapeDtypeStruct((bsz, 2, nc, LANES), BF16),
        compiler_params=_cparams(("parallel", "parallel")),
        name="nsa_compress",
    )(xc, pe, wlo, whi, w2)


def _split3(x):
    hi = x.astype(BF16)
    r1 = x - hi.astype(F32)
    mid = r1.astype(BF16)
    lo = (r1 - mid.astype(F32)).astype(BF16)
    return hi, mid, lo


def _cmp_body(q_ref, kcv_ref, mm_ref, gate_ref, ocmp_ref, sel_ref, *, tq, nc):
    t0 = pl.program_id(1) * tq
    q = q_ref[...]
    kc = kcv_ref[0]
    vc = kcv_ref[1]
    lo = _lane_lo()
    t = t0 + lax.broadcasted_iota(jnp.int32, (tq, 1), 0)
    cend = lax.broadcasted_iota(jnp.int32, (1, nc), 1) * CMP_STRIDE + (CMP_BLOCK - 1)
    bias = jnp.where(cend <= t, 0.0, NEG)
    any_valid = t >= CMP_BLOCK - 1
    gates = gate_ref[...].astype(F32)
    imp = [jnp.zeros((tq, nc), F32), jnp.zeros((tq, nc), F32)]
    for c in range(B_HEADS // 2):
        qc = q[:, c * LANES:(c + 1) * LANES]
        outs = []
        for half in range(2):
            qm = jnp.where(lo if half == 0 else ~lo, qc, jnp.zeros_like(qc))
            s = _dot_nt(qm, kc) + bias
            p = jnp.exp2(s - jnp.max(s, axis=1, keepdims=True))
            pn = p * jnp.where(any_valid, 1.0 / jnp.sum(p, axis=1, keepdims=True), 0.0)
            imp[half] = imp[half] + pn
            outs.append(_dot(pn.astype(BF16), vc))
        oc = jnp.where(lo, outs[0], outs[1])
        ocmp_ref[:, c * LANES:(c + 1) * LANES] = oc * _branch_gate(gates, 0, c, lo)

    score = jnp.zeros((tq, LANES), F32)
    for half in range(2):
        for part in _split3(imp[half]):
            score = score + _dot(part, mm_ref[half])
    st = score.T
    row = lax.broadcasted_iota(jnp.int32, (LANES, 1), 0)
    jb = jnp.where(row < SEL_BLOCK, row, row - SEL_BLOCK)
    tl = t0 + lax.broadcasted_iota(jnp.int32, (1, tq), 1)
    cur = tl // SEL_BLOCK
    forced = (jb == 0) | (jb == cur) | (jb == cur - 1)
    st = jnp.where(forced, FORCE_SCORE, jnp.where(jb * SEL_BLOCK <= tl, st, -1.0))
    sub = 8
    sub_row = lax.broadcasted_iota(jnp.int32, (sub, 1), 0)
    picked = []
    for half in range(2):
        chunks = [st[half * SEL_BLOCK + k * sub:half * SEL_BLOCK + (k + 1) * sub] for k in range(SEL_BLOCK // sub)]
        ranks = [jnp.zeros((sub, tq), F32) for _ in chunks]
        for jp in range(SEL_BLOCK):
            other = chunks[jp // sub][jp % sub:jp % sub + 1]
            for k, ch in enumerate(chunks):
                if k < jp // sub:
                    beats = other > ch
                elif k > jp // sub:
                    beats = other >= ch
                else:
                    beats = (other > ch) | ((other == ch) & (sub_row > jp % sub))
                ranks[k] = ranks[k] + jnp.where(beats, 1.0, 0.0)
        picked += [jnp.where(r < N_SELECT, 1.0, 0.0) for r in ranks]
    sel_ref[...] = jnp.concatenate(picked, axis=0)


def _cmp_select(p3, kcv, mm, g3, *, tq=256):
    bsz, seq, _ = p3.shape
    nc = kcv.shape[2]
    return pl.pallas_call(
        functools.partial(_cmp_body, tq=tq, nc=nc),
        grid=(bsz, seq // tq),
        in_specs=[
            pl.BlockSpec((None, tq, B_WIDTH), lambda b, i: (b, i, 0)),
            pl.BlockSpec((None, 2, nc, LANES), lambda b, i: (b, 0, 0, 0)),
            pl.BlockSpec((2, nc, LANES), lambda b, i: (0, 0, 0)),
            pl.BlockSpec((None, tq, LANES), lambda b, i: (b, i, G_BRANCH)),
        ],
        out_specs=[
            pl.BlockSpec((None, tq, B_WIDTH), lambda b, i: (b, i, 0)),
            pl.BlockSpec((None, 2 * SEL_BLOCK, tq), lambda b, i: (b, 0, i)),
        ],
        out_shape=[
            jax.ShapeDtypeStruct((bsz, seq, B_WIDTH), F32),
            jax.ShapeDtypeStruct((bsz, 2 * SEL_BLOCK, seq), F32),
        ],
        compiler_params=_cparams(("parallel", "parallel")),
        name="nsa_cmp_select",
    )(p3, kcv, mm, g3)


def _selwin_body(q_ref, ks_ref, vs_ref, kw_ref, vw_ref, sel_ref, ocmp_ref, gate_ref, o_ref,
                 vst_ref, vwt_ref, hot_ref, qs_ref, m_ref, acc_ref, *, tq, tk, seq):
    i = pl.program_id(1)
    t0 = i * tq
    nh = B_HEADS // 2
    nblk = tk // SEL_BLOCK
    lo = _lane_lo()
    lane = lax.broadcasted_iota(jnp.int32, (1, LANES), 1)

    @pl.when(i == 0)
    def _():
        step = 512
        for r0 in range(0, seq, step):
            vs_t = vs_ref[r0:r0 + step, :].astype(F32).T.astype(BF16)
            vw_t = vw_ref[r0:r0 + step, :].astype(F32).T.astype(BF16)
            for half in range(2):
                vst_ref[half, 0:HEAD_DIM, r0:r0 + step] = vs_t[half * HEAD_DIM:(half + 1) * HEAD_DIM]
                vwt_ref[half, 0:HEAD_DIM, r0:r0 + step] = vw_t[half * HEAD_DIM:(half + 1) * HEAD_DIM]
        ones = jnp.ones((2, ONES_ROWS, seq), BF16)
        vst_ref[:, HEAD_DIM:, :] = ones
        vwt_ref[:, HEAD_DIM:, :] = ones
        hot_ref[...] = jnp.where(lax.broadcasted_iota(jnp.int32, (tk, LANES), 0) // SEL_BLOCK
                                 == lax.broadcasted_iota(jnp.int32, (tk, LANES), 1), 1.0, 0.0).astype(BF16)

    qall = q_ref[...]
    for half in range(2):
        hm = lo if half == 0 else ~lo
        qs_ref[half] = jnp.concatenate(
            [jnp.where(hm, qall[:, c * LANES:(c + 1) * LANES], jnp.zeros((tq, LANES), BF16)) for c in range(nh)], axis=0)

    tq_lane = t0 + lax.broadcasted_iota(jnp.int32, (1, tq), 1)

    def softmax_step(st, vt, m_old):
        m_new = jnp.max(st, axis=0, keepdims=True)
        if m_old is not None:
            m_new = jnp.maximum(m_old, m_new)
        return m_new, _dot(vt, jnp.exp2(st - m_new).astype(BF16))

    span = WINDOW + tq
    wstart = pl.multiple_of(jnp.maximum(t0 - WINDOW, 0), LANES)
    dw = tq_lane - (wstart + lax.broadcasted_iota(jnp.int32, (span, 1), 0))
    wbias = jnp.where((dw >= 0) & (dw < WINDOW), 0.0, NEG)
    wbias = jnp.concatenate([wbias] * nh, axis=1)
    kw = kw_ref[pl.ds(wstart, span), :]
    o_win = []
    for half in range(2):
        _, pv = softmax_step(_dot_nt(kw, qs_ref[half]) + wbias, vwt_ref[half, :, pl.ds(wstart, span)], None)
        o_win.append(pv[:HEAD_DIM] * (1.0 / pv[HEAD_DIM:HEAD_DIM + 1]))

    m_ref[...] = jnp.full(m_ref.shape, NEG, F32)
    acc_ref[...] = jnp.zeros(acc_ref.shape, F32)
    unpicked = jnp.where(sel_ref[...].T > 0.5, 0.0, NEG)

    def tile(kt, diagonal):
        koff = pl.multiple_of(kt * tk, tk)
        ks = jnp.concatenate([ks_ref[pl.ds(koff, tk), :], hot_ref[...]], axis=1)
        for half in range(2):
            code = pltpu.roll(unpicked, (2 * LANES - half * SEL_BLOCK - kt * nblk) % LANES, axis=1)
            code = jnp.where(lane < nblk, code, 0.0).astype(BF16)
            st = _dot_nt(ks, jnp.concatenate([qs_ref[half], jnp.concatenate([code] * nh, axis=0)], axis=1))
            if diagonal:
                causal = (koff + lax.broadcasted_iota(jnp.int32, (tk, 1), 0)) <= tq_lane
                st = st + jnp.concatenate([jnp.where(causal, 0.0, NEG)] * nh, axis=1)
            m_old = m_ref[half]
            m_new, pv = softmax_step(st, vst_ref[half, :, pl.ds(koff, tk)], m_old)
            m_ref[half] = m_new
            acc_ref[half] = jnp.exp2(m_old - m_new) * acc_ref[half] + pv

    last = (t0 + tq + tk - 1) // tk - 1

    def kt_body(kt, carry):
        tile(kt, False)
        return carry

    lax.fori_loop(0, last, kt_body, 0)
    tile(last, True)

    gates = gate_ref[...].astype(F32)
    o_sel = [acc_ref[half, 0:HEAD_DIM, :] * (1.0 / acc_ref[half, HEAD_DIM:HEAD_DIM + 1, :]) for half in range(2)]
    for c in range(nh):
        cs = slice(c * LANES, (c + 1) * LANES)
        ls = slice(c * tq, (c + 1) * tq)
        osel = jnp.concatenate([o_sel[0][:, ls], o_sel[1][:, ls]], axis=0).T
        owin = jnp.concatenate([o_win[0][:, ls], o_win[1][:, ls]], axis=0).T
        o_ref[:, cs] = (ocmp_ref[:, cs] + _branch_gate(gates, 1, c, lo) * osel
                        + _branch_gate(gates, 2, c, lo) * owin).astype(o_ref.dtype)


def _sel_win(r3, p3, sel, ocmp, g3, *, tq=256, tk=1024):
    bsz, seq, _ = r3.shape
    rows = (B_HEADS // 2) * tq
    full = lambda off: pl.BlockSpec((None, seq, LANES), lambda b, i: (b, 0, off))
    tile = lambda off: pl.BlockSpec((None, tq, B_WIDTH), lambda b, i: (b, i, off))
    return pl.pallas_call(
        functools.partial(_selwin_body, tq=tq, tk=tk, seq=seq),
        grid=(bsz, seq // tq),
        in_specs=[tile(0), full(R_KSLC), full(P_VSLC), full(R_KWIN), full(P_VWIN),
                  pl.BlockSpec((None, 2 * SEL_BLOCK, tq), lambda b, i: (b, 0, i)), tile(0),
                  pl.BlockSpec((None, tq, LANES), lambda b, i: (b, i, G_BRANCH))],
        out_specs=tile(0),
        out_shape=jax.ShapeDtypeStruct((bsz, seq, B_WIDTH), BF16),
        scratch_shapes=[pltpu.VMEM((2, HEAD_DIM + ONES_ROWS, seq), BF16)] * 2
        + [pltpu.VMEM((tk, LANES), BF16), pltpu.VMEM((2, rows, LANES), BF16),
           pltpu.VMEM((2, 1, rows), F32), pltpu.VMEM((2, HEAD_DIM + ONES_ROWS, rows), F32)],
        compiler_params=_cparams(("parallel", "arbitrary")),
        name="nsa_sel_win",
    )(r3, r3, p3, r3, p3, sel, ocmp, g3)


def _stick_body(q_ref, k_ref, v_ref, o_ref, carry_ref, acc_ref, *, tq, tk):
    i = pl.program_id(2)
    t0 = i * tq
    q = q_ref[...]
    lo = _lane_lo()
    qh = [jnp.where(lo, q, jnp.zeros_like(q)), jnp.where(lo, jnp.zeros_like(q), q)]
    tpos = t0 + lax.broadcasted_iota(jnp.int32, (tq, 1), 0)
    later = jnp.where(lax.broadcasted_iota(jnp.int32, (tk, tk), 0) > lax.broadcasted_iota(jnp.int32, (tk, tk), 1),
                      1.0, 0.0).astype(BF16)
    carry_ref[...] = jnp.zeros_like(carry_ref)
    acc_ref[...] = jnp.zeros_like(acc_ref)
    ndiag = tq // tk

    def tile(koff, r0, r1, masked):
        k = k_ref[pl.ds(koff, tk), :]
        v = v_ref[pl.ds(koff, tk), :]
        rs = slice(r0, r1)
        if masked:
            before = (koff + lax.broadcasted_iota(jnp.int32, (1, tk), 1)) < tpos[rs]
        for half in range(2):
            z = _dot_nt(qh[half][rs], k)
            sp = jnp.maximum(z, 0.0) + jnp.log(1.0 + jnp.exp2(jnp.abs(z) * (-LOG2E)))
            spm = jnp.where(before, sp, 0.0) if masked else sp
            inside = _dot(spm.astype(BF16), later)
            c = carry_ref[half, rs, :]
            a = jnp.exp((z - sp) + (jnp.concatenate([c] * (tk // LANES), axis=1) - inside))
            if masked:
                a = jnp.where(before, a, 0.0)
            acc_ref[half, rs, :] += _dot(a.astype(BF16), v)
            carry_ref[half, rs, :] = c - jnp.sum(spm, axis=1, keepdims=True)

    for d in reversed(range(ndiag)):
        koff = pl.multiple_of(t0 + d * tk, tk)
        tile(koff, d * tk, (d + 1) * tk, True)
        if (d + 1) * tk < tq:
            tile(koff, (d + 1) * tk, tq, False)

    def body(n, c):
        tile(pl.multiple_of(t0 - (n + 1) * tk, tk), 0, tq, False)
        return c

    lax.fori_loop(0, i * ndiag, body, 0)
    o_ref[...] = jnp.where(lo, acc_ref[0], acc_ref[1]).astype(o_ref.dtype)


def _stick(p3, *, tq=1024, tk=256):
    bsz, seq, _ = p3.shape
    npair = C_HEADS // 2
    full = lambda off: pl.BlockSpec((None, seq, LANES), lambda b, p, i: (b, 0, off + p))
    return pl.pallas_call(
        functools.partial(_stick_body, tq=tq, tk=tk),
        grid=(bsz, npair, seq // tq),
        in_specs=[pl.BlockSpec((None, tq, LANES), lambda b, p, i: (b, i, P_CQ + p)), full(P_CK), full(P_CV)],
        out_specs=pl.BlockSpec((None, tq, LANES), lambda b, p, i: (b, i, p)),
        out_shape=jax.ShapeDtypeStruct((bsz, seq, C_WIDTH), BF16),
        scratch_shapes=[pltpu.VMEM((2, tq, LANES), F32)] * 2,
        compiler_params=_cparams(("parallel", "parallel", "parallel")),
        name="stick_breaking",
    )(p3, p3, p3)


def _merge_body(h_ref, oa_ref, ob_ref, oc_ref, ga_ref, gb_ref, gc_ref, wa_ref, wb_ref, wc_ref, wo_ref, o_ref):
    y = (ga_ref[...].astype(F32) * _dot(oa_ref[...], wa_ref[...])
         + gb_ref[...].astype(F32) * _dot(ob_ref[...], wb_ref[...])
         + gc_ref[...].astype(F32) * _dot(oc_ref[...], wc_ref[...]))
    o_ref[...] = h_ref[...] + _dot(y.astype(BF16), wo_ref[...])


def _merge(h, oa, ob, oc, gates, wa, wb, wc, wo, *, tm=512):
    n, d = h.shape
    row = lambda w: pl.BlockSpec((tm, w), lambda i: (i, 0))
    whole = lambda a: pl.BlockSpec(a.shape, lambda i: (0, 0))
    gate = lambda m: pl.BlockSpec((tm, d), lambda i: (i, m))
    return pl.pallas_call(
        _merge_body,
        grid=(n // tm,),
        in_specs=[row(d), row(A_WIDTH), row(B_WIDTH), row(C_WIDTH), gate(0), gate(1), gate(2),
                  whole(wa), whole(wb), whole(wc), whole(wo)],
        out_specs=row(d),
        out_shape=jax.ShapeDtypeStruct((n, d), F32),
        compiler_params=_cparams(("parallel",)),
        name="merge_out",
    )(h, oa, ob, oc, gates, gates, gates, wa, wb, wc, wo)


def _prep_layer(w_in, w_gate, w_up, pe_k, cw1_k, cw2_k, pe_v, cw1_v, cw2_v):
    b0 = A_IN
    kv0 = b0 + B_WIDTH
    gt0 = kv0 + B_KV_IN
    c0 = b0 + B_IN
    d = w_in.shape[0]
    nh = B_HEADS // 2
    kvcol = lambda br, kvt: kv0 + (br * 2 + kvt) * B_KV * HEAD_DIM
    span = lambda s, w: w_in[:, s:s + w]
    wq_b = (span(b0, B_WIDTH).reshape(d, 2, nh, HEAD_DIM).transpose(0, 2, 1, 3).reshape(d, B_WIDTH) * Q_SCALE2).astype(BF16)
    aw = A_GROUPS * A_WIDTH
    w_rope = jnp.concatenate([
        wq_b, (span(0, aw) * Q_SCALE2).astype(BF16), span(aw, aw).astype(BF16),
        span(kvcol(1, 0), LANES).astype(BF16), span(kvcol(2, 0), LANES).astype(BF16)], axis=1)
    w_plain = jnp.concatenate([
        wq_b, span(kvcol(0, 0), 2 * LANES).astype(BF16),
        span(kvcol(1, 1), LANES).astype(BF16), span(kvcol(2, 1), LANES).astype(BF16),
        span(2 * aw, aw).astype(BF16), (span(c0, C_WIDTH) * Q_SCALE).astype(BF16),
        span(c0 + C_WIDTH, 2 * C_WIDTH).astype(BF16)], axis=1)
    wg = span(gt0, 3 * B_HEADS).reshape(d, 2, nh, 3).transpose(0, 3, 2, 1).reshape(d, 3 * B_HEADS)
    w_sig = jnp.concatenate([w_gate.astype(BF16), wg.astype(BF16),
                             jnp.zeros((d, G_WIDTH - 3 * D_MODEL - 3 * B_HEADS), BF16)], axis=1)

    def cmp_weights(pe, w1, w2):
        w1r = w1.reshape(2, CMP_STRIDE, HEAD_DIM, CMP_HIDDEN)
        z = jnp.zeros_like(w1r)
        per_head = [jnp.stack([w1r, z], axis=2), jnp.stack([z, w1r], axis=2)]
        wfull = jnp.concatenate(per_head, axis=-1).reshape(2, CMP_STRIDE * LANES, 2 * CMP_HIDDEN)
        zz = jnp.zeros_like(w2)
        w2f = jnp.concatenate([jnp.concatenate([w2, zz], axis=1), jnp.concatenate([zz, w2], axis=1)], axis=0)
        per = pe.reshape(2, CMP_STRIDE, 1, HEAD_DIM)
        pef = jnp.broadcast_to(per, (2, CMP_STRIDE, B_KV, HEAD_DIM)).reshape(2, CMP_STRIDE * LANES)
        return pef, wfull[0].astype(BF16), wfull[1].astype(BF16), w2f.astype(BF16)

    ck, cv = cmp_weights(pe_k, cw1_k, cw2_k), cmp_weights(pe_v, cw1_v, cw2_v)
    cmp_w = tuple(jnp.stack([a, b]) for a, b in zip(ck, cv))
    w_up = w_up.astype(BF16)
    wb = w_up[A_WIDTH:A_WIDTH + B_WIDTH].reshape(2, nh, HEAD_DIM, -1).transpose(1, 0, 2, 3).reshape(B_WIDTH, -1)
    return dict(w_rope=w_rope, w_plain=w_plain, w_sig=w_sig, cmp_w=cmp_w,
                wa=w_up[:A_WIDTH], wb=wb, wc=w_up[A_WIDTH + B_WIDTH:])


def _rope_consts():
    inv = ROPE_THETA ** (-jnp.arange(0, ROT_DIM, 2, dtype=F32) / ROT_DIM)
    lane = np.arange(LANES) % HEAD_DIM
    half = ROT_DIM // 2
    inv_l = jnp.where(lane < ROT_DIM, inv[lane % half], 0.0).reshape(1, LANES).astype(F32)
    sa = np.where(lane < half, -1.0, 0.0).reshape(1, LANES).astype(np.float32)
    sb = np.where((lane >= half) & (lane < ROT_DIM), 1.0, 0.0).reshape(1, LANES).astype(np.float32)
    return inv_l, jnp.asarray(sa), jnp.asarray(sb)


def _block_sum_matrix(nc):
    ratio = SEL_BLOCK // CMP_STRIDE
    per = CMP_BLOCK // CMP_STRIDE
    m = np.zeros((2, nc, LANES), np.float32)
    for j in range(SEL_BLOCK):
        for a in range(ratio):
            for b in range(per):
                c = ratio * j + a + b
                if c < nc - 1:
                    m[0, c, j] += 1.0
                    m[1, c, SEL_BLOCK + j] += 1.0
    return jnp.asarray(m).astype(BF16)


def _mixers(h, lw, norm_mix, rope_in, bsz, seq):
    n = bsz * seq
    r, p, g, xc = _proj(h, norm_mix, lw["w_rope"], lw["w_plain"], lw["w_sig"], rope_in)
    r3, p3, g3 = r.reshape(bsz, seq, R_WIDTH), p.reshape(bsz, seq, P_WIDTH), g.reshape(bsz, seq, G_WIDTH)
    oa = _attn_a(r3, p3)
    nc = seq // CMP_STRIDE
    kcv = _compress(xc.reshape(2, bsz, nc, CMP_STRIDE * LANES), *lw["cmp_w"])
    ocmp, sel = _cmp_select(p3, kcv, _block_sum_matrix(nc), g3)
    ob = _sel_win(r3, p3, sel, ocmp, g3)
    oc = _stick(p3)
    return oa.reshape(n, A_WIDTH), ob.reshape(n, B_WIDTH), oc.reshape(n, C_WIDTH), g


def kernel(x, positions, norm_ffn1, ffn1_w1, ffn1_w3, ffn1_w2, norm_mix, w_in, cmp_pe_k, cmp_w1_k, cmp_w2_k,
           cmp_pe_v, cmp_w1_v, cmp_w2_v, w_gate, w_up, w_out, norm_ffn2, ffn2_w1, ffn2_w3, ffn2_w2, norm_final):
    bsz, seq, d = x.shape
    n = bsz * seq
    depth = w_in.shape[0]
    rope_in = (positions.astype(F32).reshape(n, 1),) + _rope_consts()
    h = x.reshape(n, d)
    for i in range(depth):
        lw = _prep_layer(w_in[i], w_gate[i], w_up[i], cmp_pe_k[i], cmp_w1_k[i], cmp_w2_k[i],
                         cmp_pe_v[i], cmp_w1_v[i], cmp_w2_v[i])
        h = _ffn(h, norm_ffn1[i], ffn1_w1[i].astype(BF16), ffn1_w3[i].astype(BF16), ffn1_w2[i].astype(BF16))
        oa, ob, oc, g = _mixers(h, lw, norm_mix[i], rope_in, bsz, seq)
        h = _merge(h, oa, ob, oc, g, lw["wa"], lw["wb"], lw["wc"], w_out[i].astype(BF16))
        h = _ffn(h, norm_ffn2[i], ffn2_w1[i].astype(BF16), ffn2_w3[i].astype(BF16), ffn2_w2[i].astype(BF16),
                 norm_final if i == depth - 1 else None)
    return h.reshape(bsz, seq, d)
```
